```python
import math
import jax, jax.numpy as jnp
from jax import lax
import numpy as np

D_MODEL = 2048
BATCH = 8
SEQ = 2048
DEPTH = 2

CHUNK = 64
HEAD_DIM = 128
EPS = 1e-6
A_HEADS = D_MODEL // HEAD_DIM
A_LEFT_CHUNKS = 8
A_BAND = (A_LEFT_CHUNKS + 1) * CHUNK
A_MAX_REL = 128
A_WIDTH = A_HEADS * HEAD_DIM
B_HEADS = D_MODEL // (2 * HEAD_DIM)
B_QK_WIDTH = B_HEADS * 2 * HEAD_DIM
B_V_WIDTH = B_HEADS * 2 * HEAD_DIM
Q_BLOCK = 128
C_WIDTH = D_MODEL
C_GROUPS = 16
CONV_K = 3
N_BRANCH = 3
D_FF = 4 * D_MODEL
IN_COLS = 3 * A_WIDTH + 2 * B_QK_WIDTH + B_V_WIDTH + 3 * C_WIDTH

kernel_name = "chunk_causal_hybrid_gated_trunk"


def _rms(x, g):
    xf = x.astype(jnp.float32)
    y = xf * lax.rsqrt(jnp.mean(xf * xf, axis=-1, keepdims=True) + EPS)
    return (y * g.astype(jnp.float32)).astype(x.dtype)


def _chunked_relpos_attention(q, k, v, rel_bias):
    B, S, H, d = q.shape
    n_chunks = S // CHUNK
    pad = A_LEFT_CHUNKS * CHUNK
    kp = jnp.pad(k, ((0, 0), (pad, 0), (0, 0), (0, 0)))
    vp = jnp.pad(v, ((0, 0), (pad, 0), (0, 0), (0, 0)))
    qc = q.reshape(B, n_chunks, CHUNK, H, d)
    i_idx = jnp.arange(CHUNK)[:, None]
    j_idx = jnp.arange(A_BAND)
    rel = jnp.clip(i_idx + pad - j_idx[None, :], -A_MAX_REL, A_MAX_REL) + A_MAX_REL
    bias = rel_bias[:, rel].astype(jnp.float32)
    scale = d ** -0.5

    def one_chunk(c):
        q_c = lax.dynamic_index_in_dim(qc, c, axis=1, keepdims=False)
        k_b = lax.dynamic_slice_in_dim(kp, c * CHUNK, A_BAND, axis=1)
        v_b = lax.dynamic_slice_in_dim(vp, c * CHUNK, A_BAND, axis=1)
        s = jnp.einsum('bqhd,bkhd->bhqk', q_c, k_b,
                       preferred_element_type=jnp.float32) * scale + bias[None]
        valid = j_idx >= pad - c * CHUNK
        s = jnp.where(valid[None, None, None, :], s, -jnp.inf)
        p = jax.nn.softmax(s, axis=-1).astype(v.dtype)
        return jnp.einsum('bhqk,bkhd->bqhd', p, v_b)

    out = lax.map(one_chunk, jnp.arange(n_chunks))
    return out.transpose(1, 0, 2, 3, 4).reshape(B, S, H * d)


def _differential_attention(q1, q2, k1, k2, v, lam):
    B, S, H, d = q1.shape
    n_blocks = S // Q_BLOCK
    key_chunk = jnp.arange(S) // CHUNK
    scale = d ** -0.5
    q1b = q1.reshape(B, n_blocks, Q_BLOCK, H, d)
    q2b = q2.reshape(B, n_blocks, Q_BLOCK, H, d)

    def one_block(n):
        a1 = lax.dynamic_index_in_dim(q1b, n, axis=1, keepdims=False)
        a2 = lax.dynamic_index_in_dim(q2b, n, axis=1, keepdims=False)
        q_chunk = (n * Q_BLOCK + jnp.arange(Q_BLOCK)) // CHUNK
        mask = (key_chunk[None, :] <= q_chunk[:, None])[None, None]
        s1 = jnp.einsum('bqhd,bkhd->bhqk', a1, k1, preferred_element_type=jnp.float32) * scale
        s2 = jnp.einsum('bqhd,bkhd->bhqk', a2, k2, preferred_element_type=jnp.float32) * scale
        p1 = jax.nn.softmax(jnp.where(mask, s1, -jnp.inf), axis=-1)
        p2 = jax.nn.softmax(jnp.where(mask, s2, -jnp.inf), axis=-1)
        p = (p1 - lam * p2).astype(v.dtype)
        return jnp.einsum('bhqk,bkhe->bqhe', p, v)

    out = lax.map(one_block, jnp.arange(n_blocks))
    return out.transpose(1, 0, 2, 3, 4).reshape(B, S, H, 2 * d)


def _short_gated_conv(bg, cg, xin, conv_w):
    u = cg * xin
    y = lax.conv_general_dilated(u, conv_w[:, None, :].astype(u.dtype), window_strides=(1,),
                                 padding=[(CONV_K - 1, 0)],
                                 dimension_numbers=('NWC', 'WIO', 'NWC'),
                                 feature_group_count=C_WIDTH)
    return bg * y


def setup_inputs(seed: int = 0) -> dict:
    key = jax.random.key(seed)
    ks = jax.random.split(key, 24)
    f32 = jnp.float32

    def nrm(k, shape, scale):
        return jax.random.normal(k, shape, f32) * scale

    D = D_MODEL
    return {
        "x": nrm(ks[0], (BATCH, SEQ, D), 1.0),
        "norm1_g": 1.0 + nrm(ks[1], (DEPTH, D), 0.02),
        "w_in": nrm(ks[2], (DEPTH, D, IN_COLS), D ** -0.5),
        "w_gate": nrm(ks[3], (DEPTH, D, N_BRANCH * D), D ** -0.5),
        "b_gate": nrm(ks[4], (DEPTH, N_BRANCH * D), 0.01),
        "a_qn_g": 1.0 + nrm(ks[5], (DEPTH, HEAD_DIM), 0.02),
        "a_kn_g": 1.0 + nrm(ks[6], (DEPTH, HEAD_DIM), 0.02),
        "a_rel_bias": nrm(ks[7], (DEPTH, A_HEADS, 2 * A_MAX_REL + 1), 0.1),
        "b_qn_g": 1.0 + nrm(ks[8], (DEPTH, HEAD_DIM), 0.02),
        "b_kn_g": 1.0 + nrm(ks[9], (DEPTH, HEAD_DIM), 0.02),
        "b_lq1": nrm(ks[10], (DEPTH, HEAD_DIM), 0.1),
        "b_lk1": nrm(ks[11], (DEPTH, HEAD_DIM), 0.1),
        "b_lq2": nrm(ks[12], (DEPTH, HEAD_DIM), 0.1),
        "b_lk2": nrm(ks[13], (DEPTH, HEAD_DIM), 0.1),
        "b_subln_g": 1.0 + nrm(ks[14], (DEPTH, 2 * HEAD_DIM), 0.02),
        "c_conv_w": nrm(ks[15], (DEPTH, CONV_K, C_WIDTH), CONV_K ** -0.5),
        "w_o": nrm(ks[16], (DEPTH, D, D), D ** -0.5),
        "norm2_g": 1.0 + nrm(ks[17], (DEPTH, D), 0.02),
        "w_mlp1": nrm(ks[18], (DEPTH, D, D_FF), D ** -0.5),
        "w_mlp2": nrm(ks[19], (DEPTH, D_FF, D), D_FF ** -0.5),
    }


def reference(x, norm1_g, w_in, w_gate, b_gate, a_qn_g, a_kn_g, a_rel_bias,
              b_qn_g, b_kn_g, b_lq1, b_lk1, b_lq2, b_lk2, b_subln_g, c_conv_w,
              w_o, norm2_g, w_mlp1, w_mlp2):
    B, S, _ = x.shape
    sizes = [A_WIDTH, A_WIDTH, A_WIDTH, B_QK_WIDTH, B_QK_WIDTH, B_V_WIDTH,
             C_WIDTH, C_WIDTH, C_WIDTH]
    offsets = [int(o) for o in np.cumsum(sizes)[:-1]]
    for l in range(DEPTH):
        h = _rms(x, norm1_g[l])
        proj = h @ w_in[l]
        qa, ka, va, qb, kb, vb, bg, cg, xc = jnp.split(proj, offsets, axis=-1)

        qa = _rms(qa.reshape(B, S, A_HEADS, HEAD_DIM), a_qn_g[l])
        ka = _rms(ka.reshape(B, S, A_HEADS, HEAD_DIM), a_kn_g[l])
        va = va.reshape(B, S, A_HEADS, HEAD_DIM)
        out_a = _chunked_relpos_attention(qa, ka, va, a_rel_bias[l])

        qb = _rms(qb.reshape(B, S, B_HEADS, 2, HEAD_DIM), b_qn_g[l])
        kb = _rms(kb.reshape(B, S, B_HEADS, 2, HEAD_DIM), b_kn_g[l])
        vb = vb.reshape(B, S, B_HEADS, 2 * HEAD_DIM)
        lambda_init = 0.8 - 0.6 * math.exp(-0.3 * l)
        lam = (jnp.exp(jnp.sum(b_lq1[l].astype(jnp.float32) * b_lk1[l].astype(jnp.float32)))
               - jnp.exp(jnp.sum(b_lq2[l].astype(jnp.float32) * b_lk2[l].astype(jnp.float32)))
               + lambda_init)
        ob = _differential_attention(qb[..., 0, :], qb[..., 1, :], kb[..., 0, :], kb[..., 1, :], vb, lam)
        out_b = (_rms(ob, b_subln_g[l]) * (1.0 - lambda_init)).reshape(B, S, B_V_WIDTH)

        out_c = _short_gated_conv(bg, cg, xc, c_conv_w[l])

        gates = jax.nn.sigmoid(h @ w_gate[l] + b_gate[l])
        g_a, g_b, g_c = jnp.split(gates, N_BRANCH, axis=-1)
        x = x + (g_a * out_a + g_b * out_b + g_c * out_c) @ w_o[l]

        h2 = _rms(x, norm2_g[l])
        x = x + jnp.square(jax.nn.relu(h2 @ w_mlp1[l])) @ w_mlp2[l]
    return x
```

```python
import functools
import math

import jax
import jax.numpy as jnp
from jax import lax
from jax.experimental import pallas as pl
from jax.experimental.pallas import tpu as pltpu

F32 = jnp.float32
BF16 = jnp.bfloat16

LANE = 128
BF16_SUBLANES = 16
CHUNK = 64
HEAD_DIM = 128
EPS = 1e-6
A_LEFT_CHUNKS = 8
A_MAX_REL = 128
A_QBLOCK = 2 * CHUNK
A_BAND = (A_LEFT_CHUNKS + 2) * CHUNK
B_QBLOCK = 256
MASKED = -1e30
VMEM_LIMIT = 56 * 1024 * 1024


def _params(*sem):
    return pltpu.CompilerParams(dimension_semantics=sem, vmem_limit_bytes=VMEM_LIMIT)


def _tile(n, pref):
    if n <= pref:
        return n
    t = pref
    while n % t:
        t -= LANE
    assert t > 0
    return t


def _rmsnorm_kernel(x_ref, g_ref, o_ref):
    x = x_ref[...]
    ms = jnp.mean(x * x, axis=-1, keepdims=True)
    o_ref[...] = (x * lax.rsqrt(ms + EPS) * g_ref[...]).astype(o_ref.dtype)


def _rmsnorm(x, g):
    T, D = x.shape
    tm = _tile(T, 512)
    return pl.pallas_call(
        _rmsnorm_kernel,
        grid=(T // tm,),
        in_specs=[pl.BlockSpec((tm, D), lambda i: (i, 0)),
                  pl.BlockSpec((1, D), lambda i: (0, 0))],
        out_specs=pl.BlockSpec((tm, D), lambda i: (i, 0)),
        out_shape=jax.ShapeDtypeStruct((T, D), BF16),
        compiler_params=_params("parallel"),
        name="rmsnorm",
    )(x, g.reshape(1, D).astype(F32))


def _proj_heads_kernel(h_ref, w_ref, gain_ref, o_ref, *, tiles_per_seg, plain_segs):
    acc = jnp.dot(h_ref[...], w_ref[...], preferred_element_type=F32)
    n_groups = o_ref.shape[0]
    seg = pl.program_id(1) // tiles_per_seg
    is_plain = functools.reduce(jnp.logical_or, [seg == s for s in plain_segs])

    @pl.when(is_plain)
    def _():
        for g in range(n_groups):
            o_ref[g] = acc[:, g * LANE:(g + 1) * LANE].astype(o_ref.dtype)

    @pl.when(jnp.logical_not(is_plain))
    def _():
        gain = gain_ref[...]
        for g in range(n_groups):
            a = acc[:, g * LANE:(g + 1) * LANE]
            ms = jnp.mean(a * a, axis=-1, keepdims=True)
            o_ref[g] = (a * lax.rsqrt(ms + EPS) * gain).astype(o_ref.dtype)


def _proj_heads(h, w, gains, *, n_cols, seg_cols, plain_segs):
    T, D = h.shape
    tm = _tile(T, 1024)
    tn = _tile(seg_cols, 1024)
    tiles_per_seg = seg_cols // tn
    kern = functools.partial(_proj_heads_kernel, tiles_per_seg=tiles_per_seg, plain_segs=plain_segs)
    return pl.pallas_call(
        kern,
        grid=(T // tm, n_cols // tn),
        in_specs=[pl.BlockSpec((tm, D), lambda i, j: (i, 0)),
                  pl.BlockSpec((D, tn), lambda i, j: (0, j)),
                  pl.BlockSpec((None, 1, LANE), lambda i, j: (j // tiles_per_seg, 0, 0))],
        out_specs=pl.BlockSpec((tn // LANE, tm, LANE), lambda i, j: (j, i, 0)),
        out_shape=jax.ShapeDtypeStruct((n_cols // LANE, T, LANE), BF16),
        compiler_params=_params("parallel", "arbitrary"),
        name="proj_heads",
    )(h, w, gains)


def _proj_flat_kernel(h_ref, w_ref, b_ref, o_ref, *, gate):
    acc = jnp.dot(h_ref[...], w_ref[...], preferred_element_type=F32)
    if gate:
        acc = jax.nn.sigmoid(acc + b_ref[...])
    o_ref[...] = acc.astype(o_ref.dtype)


def _proj_flat(h, w, b, *, col0, n_cols, gate, name):
    T, D = h.shape
    tm = _tile(T, 1024)
    tn = _tile(n_cols, 1024)
    assert col0 % tn == 0
    j0 = col0 // tn
    return pl.pallas_call(
        functools.partial(_proj_flat_kernel, gate=gate),
        grid=(T // tm, n_cols // tn),
        in_specs=[pl.BlockSpec((tm, D), lambda i, j: (i, 0)),
                  pl.BlockSpec((D, tn), lambda i, j: (0, j + j0)),
                  pl.BlockSpec((1, tn), lambda i, j: (0, j))],
        out_specs=pl.BlockSpec((tm, tn), lambda i, j: (i, j)),
        out_shape=jax.ShapeDtypeStruct((T, n_cols), BF16),
        compiler_params=_params("parallel", "arbitrary"),
        name=name,
    )(h, w, b)


def _attn_a_bias_tile(rel_bias):
    i = jnp.arange(A_QBLOCK)[:, None]
    j = jnp.arange(A_BAND)[None, :]
    qc, kc = i // CHUNK, j // CHUNK
    dist = (qc + A_LEFT_CHUNKS - kc) * CHUNK + (i % CHUNK) - (j % CHUNK)
    valid = (kc >= qc) & (kc <= qc + A_LEFT_CHUNKS)
    idx = jnp.clip(dist, -A_MAX_REL, A_MAX_REL) + A_MAX_REL
    return jnp.where(valid[None], rel_bias.astype(F32)[:, idx], MASKED)


def _attn_a_kernel(q_ref, k_ref, v_ref, bias_ref, o_ref):
    S = q_ref.shape[0]
    for blk in range(S // A_QBLOCK):
        q0 = blk * A_QBLOCK
        k1 = q0 + A_QBLOCK
        k0 = max(0, k1 - A_BAND)
        b0 = A_BAND - (k1 - k0)
        s = lax.dot_general(q_ref[q0:k1, :], k_ref[k0:k1, :], (((1,), (1,)), ((), ())),
                            preferred_element_type=F32)
        s = s + bias_ref[:, b0:]
        e = jnp.exp(s - jnp.max(s, axis=-1, keepdims=True))
        l = jnp.sum(e, axis=-1, keepdims=True)
        o = jnp.dot(e.astype(BF16), v_ref[k0:k1, :], preferred_element_type=F32)
        o_ref[q0:k1, :] = (o / l).astype(o_ref.dtype)


def _attn_a(p_heads, bias_tile, *, batch, seq, heads):
    T = p_heads.shape[1]

    def spec(off):
        return pl.BlockSpec((None, seq, HEAD_DIM), lambda b, h: (h + off, b, 0))

    return pl.pallas_call(
        _attn_a_kernel,
        grid=(batch, heads),
        in_specs=[spec(0), spec(heads), spec(2 * heads),
                  pl.BlockSpec((None, A_QBLOCK, A_BAND), lambda b, h: (h, 0, 0))],
        out_specs=pl.BlockSpec((None, seq, HEAD_DIM), lambda b, h: (h, b, 0)),
        out_shape=jax.ShapeDtypeStruct((heads, T, HEAD_DIM), BF16),
        compiler_params=_params("parallel", "parallel"),
        name="attn_a",
    )(p_heads, p_heads, p_heads, bias_tile)


def _attn_b_kernel(lam_ref, q_ref, k_ref, v_ref, g_ref, o_ref, *, out_scale):
    S = q_ref.shape[1]
    lam = lam_ref[0]
    v = jnp.concatenate([v_ref[0], v_ref[1]], axis=-1)
    row_chunk = lax.broadcasted_iota(jnp.int32, (B_QBLOCK, B_QBLOCK), 0) // CHUNK
    col_chunk = lax.broadcasted_iota(jnp.int32, (B_QBLOCK, B_QBLOCK), 1) // CHUNK
    diag_mask = jnp.where(col_chunk <= row_chunk, 0.0, MASKED).astype(F32)
    nt = (((1,), (1,)), ((), ()))

    for blk in range(S // B_QBLOCK):
        q0 = blk * B_QBLOCK
        q1 = q0 + B_QBLOCK

        def softmax_parts(m):
            q = q_ref[m, q0:q1, :]
            sd = lax.dot_general(q, k_ref[m, q0:q1, :], nt, preferred_element_type=F32) + diag_mask
            mx = jnp.max(sd, axis=-1, keepdims=True)
            if blk:
                sl = lax.dot_general(q, k_ref[m, 0:q0, :], nt, preferred_element_type=F32)
                mx = jnp.maximum(mx, jnp.max(sl, axis=-1, keepdims=True))
                el = jnp.exp(sl - mx)
            ed = jnp.exp(sd - mx)
            l = jnp.sum(ed, axis=-1, keepdims=True)
            if blk:
                l = l + jnp.sum(el, axis=-1, keepdims=True)
                return el, ed, l
            return None, ed, l

        el1, ed1, l1 = softmax_parts(0)
        el2, ed2, l2 = softmax_parts(1)
        c1 = 1.0 / l1
        c2 = lam / l2
        pd = (ed1 * c1 - ed2 * c2).astype(BF16)
        ob = jnp.dot(pd, v[q0:q1, :], preferred_element_type=F32)
        if blk:
            pl_ = (el1 * c1 - el2 * c2).astype(BF16)
            ob = ob + jnp.dot(pl_, v[0:q0, :], preferred_element_type=F32)
        ms = jnp.mean(ob * ob, axis=-1, keepdims=True)
        o_ref[q0:q1, :] = (ob * lax.rsqrt(ms + EPS) * g_ref[...] * out_scale).astype(o_ref.dtype)


def _attn_b(p_heads, lam, subln_g, *, batch, seq, heads, q_grp, k_grp, v_grp, out_scale):
    T = p_heads.shape[1]

    def spec(grp):
        return pl.BlockSpec((2, seq, HEAD_DIM), lambda b, h: (h + grp // 2, b, 0))

    return pl.pallas_call(
        functools.partial(_attn_b_kernel, out_scale=out_scale),
        grid=(batch, heads),
        in_specs=[pl.BlockSpec(memory_space=pltpu.SMEM),
                  spec(q_grp), spec(k_grp), spec(v_grp),
                  pl.BlockSpec((1, 2 * HEAD_DIM), lambda b, h: (0, 0))],
        out_specs=pl.BlockSpec((None, seq, 2 * HEAD_DIM), lambda b, h: (h, b, 0)),
        out_shape=jax.ShapeDtypeStruct((heads, T, 2 * HEAD_DIM), BF16),
        compiler_params=_params("parallel", "parallel"),
        name="attn_b",
    )(lam, p_heads, p_heads, p_heads, subln_g.reshape(1, 2 * HEAD_DIM).astype(F32))


def _merge_out_kernel(x_ref, oa_ref, ob_ref, bg_ref, cg_ref, xc_ref, cgh_ref, xch_ref,
                      ga_ref, gb_ref, gc_ref, cw_ref, wo_ref, g2_ref, x1_ref, h2_ref, m_ref, *, seq):
    tm, D = x_ref.shape
    f = lambda r: r.astype(F32)
    starts_seq = (pl.program_id(0) * tm) % seq == 0
    row = lax.broadcasted_iota(jnp.int32, (tm, 1), 0)
    for g in range(D // LANE):
        sl = slice(g * LANE, (g + 1) * LANE)
        u = f(cg_ref[:, sl]) * f(xc_ref[:, sl])
        halo = jnp.where(starts_seq, 0.0, f(cgh_ref[:, sl]) * f(xch_ref[:, sl]))
        prev1 = halo[BF16_SUBLANES - 1:BF16_SUBLANES, :]
        prev2 = halo[BF16_SUBLANES - 2:BF16_SUBLANES - 1, :]
        u1 = jnp.where(row == 0, prev1, pltpu.roll(u, 1, 0))
        u2 = jnp.where(row == 0, prev2, jnp.where(row == 1, prev1, pltpu.roll(u, 2, 0)))
        conv = cw_ref[0:1, sl] * u2 + cw_ref[1:2, sl] * u1 + cw_ref[2:3, sl] * u
        half = slice((g % 2) * LANE, (g % 2 + 1) * LANE)
        m = (f(ga_ref[:, sl]) * f(oa_ref[g]) + f(gb_ref[:, sl]) * f(ob_ref[g // 2, :, half])
             + f(gc_ref[:, sl]) * (f(bg_ref[:, sl]) * conv))
        m_ref[:, sl] = m.astype(m_ref.dtype)
    x1 = x_ref[...] + jnp.dot(m_ref[...], wo_ref[...], preferred_element_type=F32)
    x1_ref[...] = x1
    ms = jnp.mean(x1 * x1, axis=-1, keepdims=True)
    h2_ref[...] = (x1 * lax.rsqrt(ms + EPS) * g2_ref[...]).astype(h2_ref.dtype)


def _merge_out(x, out_a, out_b, flat, gates, conv_w, w_o, g2, *, seq):
    T, D = x.shape
    tm = _tile(seq, 256)
    halo_blocks = tm // BF16_SUBLANES

    def col(c):
        return pl.BlockSpec((tm, D), lambda i: (i, c))

    def halo(c):
        return pl.BlockSpec((BF16_SUBLANES, D), lambda i: (jnp.maximum(i * halo_blocks - 1, 0), c))

    return pl.pallas_call(
        functools.partial(_merge_out_kernel, seq=seq),
        grid=(T // tm,),
        in_specs=[pl.BlockSpec((tm, D), lambda i: (i, 0)),
                  pl.BlockSpec((out_a.shape[0], tm, HEAD_DIM), lambda i: (0, i, 0)),
                  pl.BlockSpec((out_b.shape[0], tm, 2 * HEAD_DIM), lambda i: (0, i, 0)),
                  col(0), col(1), col(2), halo(1), halo(2),
                  col(0), col(1), col(2),
                  pl.BlockSpec((3, D), lambda i: (0, 0)),
                  pl.BlockSpec((D, D), lambda i: (0, 0)),
                  pl.BlockSpec((1, D), lambda i: (0, 0))],
        out_specs=[pl.BlockSpec((tm, D), lambda i: (i, 0)),
                   pl.BlockSpec((tm, D), lambda i: (i, 0))],
        out_shape=[jax.ShapeDtypeStruct((T, D), F32), jax.ShapeDtypeStruct((T, D), BF16)],
        scratch_shapes=[pltpu.VMEM((tm, D), BF16)],
        compiler_params=_params("parallel"),
        name="merge_out",
    )(x, out_a, out_b, flat, flat, flat, flat, flat, gates, gates, gates,
      conv_w.astype(F32), w_o, g2.reshape(1, D).astype(F32))


def _mlp_kernel(x1_ref, h2_ref, w1_ref, w2_ref, gn_ref, x2_ref, *hn_ref, with_next):
    f = pl.program_id(1)
    a = jnp.dot(h2_ref[...], w1_ref[...], preferred_element_type=F32)
    a = jnp.square(jnp.maximum(a, 0.0)).astype(BF16)
    d = jnp.dot(a, w2_ref[...], preferred_element_type=F32)

    @pl.when(f == 0)
    def _():
        x2_ref[...] = x1_ref[...] + d

    @pl.when(f > 0)
    def _():
        x2_ref[...] += d

    if with_next:
        @pl.when(f == pl.num_programs(1) - 1)
        def _():
            x2 = x2_ref[...]
            ms = jnp.mean(x2 * x2, axis=-1, keepdims=True)
            hn_ref[0][...] = (x2 * lax.rsqrt(ms + EPS) * gn_ref[...]).astype(BF16)


def _mlp(x1, h2, w1, w2, g_next):
    T, D = x1.shape
    FF = w1.shape[1]
    tm = _tile(T, 512)
    tf = _tile(FF, 512)
    with_next = g_next is not None
    gn = (g_next if with_next else jnp.ones((D,), F32)).reshape(1, D).astype(F32)
    row = lambda i, f: (i, 0)
    out_specs = [pl.BlockSpec((tm, D), row)]
    out_shape = [jax.ShapeDtypeStruct((T, D), F32)]
    if with_next:
        out_specs.append(pl.BlockSpec((tm, D), row))
        out_shape.append(jax.ShapeDtypeStruct((T, D), BF16))
    out = pl.pallas_call(
        functools.partial(_mlp_kernel, with_next=with_next),
        grid=(T // tm, FF // tf),
        in_specs=[pl.BlockSpec((tm, D), row),
                  pl.BlockSpec((tm, D), row),
                  pl.BlockSpec((D, tf), lambda i, f: (0, f)),
                  pl.BlockSpec((tf, D), lambda i, f: (f, 0)),
                  pl.BlockSpec((1, D), lambda i, f: (0, 0))],
        out_specs=out_specs,
        out_shape=out_shape,
        compiler_params=_params("parallel", "arbitrary"),
        name="mlp",
    )(x1, h2, w1, w2, gn)
    return (out[0], out[1]) if with_next else (out[0], None)


def kernel(x, norm1_g, w_in, w_gate, b_gate, a_qn_g, a_kn_g, a_rel_bias, b_qn_g, b_kn_g,
           b_lq1, b_lk1, b_lq2, b_lk2, b_subln_g, c_conv_w, w_o, norm2_g, w_mlp1, w_mlp2):
    B, S, D = x.shape
    depth = w_in.shape[0]
    T = B * S
    a_heads = D // HEAD_DIM
    b_heads = D // (2 * HEAD_DIM)
    assert S % B_QBLOCK == 0 and D % (2 * HEAD_DIM) == 0 and w_in.shape[2] == 9 * D
    scale = HEAD_DIM ** -0.5
    ones = jnp.ones((HEAD_DIM,), F32)

    xf = x.reshape(T, D).astype(F32)
    w_in_b, w_gate_b, w_o_b = w_in.astype(BF16), w_gate.astype(BF16), w_o.astype(BF16)
    w1_b, w2_b = w_mlp1.astype(BF16), w_mlp2.astype(BF16)

    h = _rmsnorm(xf, norm1_g[0])
    for l in range(depth):
        gains = jnp.stack([a_qn_g[l].astype(F32) * scale, a_kn_g[l].astype(F32), ones,
                           b_qn_g[l].astype(F32) * scale, b_kn_g[l].astype(F32), ones]).reshape(6, 1, HEAD_DIM)
        p_heads = _proj_heads(h, w_in_b[l], gains, n_cols=6 * D, seg_cols=D, plain_segs=(2, 5))
        flat = _proj_flat(h, w_in_b[l], jnp.zeros((1, 3 * D), F32), col0=6 * D, n_cols=3 * D,
                          gate=False, name="proj_flat")
        gates = _proj_flat(h, w_gate_b[l], b_gate[l].reshape(1, 3 * D).astype(F32), col0=0, n_cols=3 * D,
                           gate=True, name="proj_gate")

        out_a = _attn_a(p_heads, _attn_a_bias_tile(a_rel_bias[l]), batch=B, seq=S, heads=a_heads)

        lambda_init = 0.8 - 0.6 * math.exp(-0.3 * l)
        lam = (jnp.exp(jnp.sum(b_lq1[l].astype(F32) * b_lk1[l].astype(F32)))
               - jnp.exp(jnp.sum(b_lq2[l].astype(F32) * b_lk2[l].astype(F32))) + lambda_init).reshape(1)
        out_b = _attn_b(p_heads, lam, b_subln_g[l], batch=B, seq=S, heads=b_heads,
                        q_grp=3 * a_heads, k_grp=4 * a_heads, v_grp=5 * a_heads,
                        out_scale=1.0 - lambda_init)

        x1, h2 = _merge_out(xf, out_a, out_b, flat, gates, c_conv_w[l], w_o_b[l], norm2_g[l], seq=S)
        xf, h = _mlp(x1, h2, w1_b[l], w2_b[l], norm1_g[l + 1] if l + 1 < depth else None)
    return xf.reshape(B, S, D).astype(x.dtype)
```

```python
import functools
import math

import jax
import jax.numpy as jnp
from jax import lax
from jax.experimental import pallas as pl
from jax.experimental.pallas import tpu as pltpu

F32 = jnp.float32
BF16 = jnp.bfloat16

LANE = 128
BF16_SUBLANES = 16
CHUNK = 64
HEAD_DIM = 128
EPS = 1e-6
A_LEFT_CHUNKS = 8
A_MAX_REL = 128
A_QBLOCK = 2 * CHUNK
A_BAND = (A_LEFT_CHUNKS + 2) * CHUNK
B_QBLOCK = 256
MASKED = -1e30
VMEM_LIMIT = 56 * 1024 * 1024


def _params(*sem):
    return pltpu.CompilerParams(dimension_semantics=sem, vmem_limit_bytes=VMEM_LIMIT)


def _tile(n, pref):
    if n <= pref:
        return n
    t = pref
    while n % t:
        t -= LANE
    assert t > 0
    return t


def _rmsnorm_kernel(x_ref, g_ref, o_ref):
    x = x_ref[...]
    ms = jnp.mean(x * x, axis=-1, keepdims=True)
    o_ref[...] = (x * lax.rsqrt(ms + EPS) * g_ref[...]).astype(o_ref.dtype)


def _rmsnorm(x, g):
    T, D = x.shape
    tm = _tile(T, 512)
    return pl.pallas_call(
        _rmsnorm_kernel,
        grid=(T // tm,),
        in_specs=[pl.BlockSpec((tm, D), lambda i: (i, 0)),
                  pl.BlockSpec((1, D), lambda i: (0, 0))],
        out_specs=pl.BlockSpec((tm, D), lambda i: (i, 0)),
        out_shape=jax.ShapeDtypeStruct((T, D), BF16),
        compiler_params=_params("parallel"),
        name="rmsnorm",
    )(x, g.reshape(1, D).astype(F32))


def _proj_heads_kernel(h_ref, w_ref, gain_ref, o_ref, *, tiles_per_seg, plain_segs):
    acc = jnp.dot(h_ref[...], w_ref[...], preferred_element_type=F32)
    n_groups = o_ref.shape[0]
    seg = pl.program_id(1) // tiles_per_seg
    is_plain = functools.reduce(jnp.logical_or, [seg == s for s in plain_segs])

    @pl.when(is_plain)
    def _():
        for g in range(n_groups):
            o_ref[g] = acc[:, g * LANE:(g + 1) * LANE].astype(o_ref.dtype)

    @pl.when(jnp.logical_not(is_plain))
    def _():
        gain = gain_ref[...]
        for g in range(n_groups):
            a = acc[:, g * LANE:(g + 1) * LANE]
            ms = jnp.mean(a * a, axis=-1, keepdims=True)
            o_ref[g] = (a * lax.rsqrt(ms + EPS) * gain).astype(o_ref.dtype)


def _proj_heads(h, w, layer, gains, *, n_cols, seg_cols, plain_segs):
    T, D = h.shape
    tm = _tile(T, 1024)
    tn = _tile(seg_cols, 1024)
    tiles_per_seg = seg_cols // tn
    kern = functools.partial(_proj_heads_kernel, tiles_per_seg=tiles_per_seg, plain_segs=plain_segs)
    return pl.pallas_call(
        kern,
        grid=(T // tm, n_cols // tn),
        in_specs=[pl.BlockSpec((tm, D), lambda i, j: (i, 0)),
                  pl.BlockSpec((None, D, tn), lambda i, j: (layer, 0, j)),
                  pl.BlockSpec((None, 1, LANE), lambda i, j: (j // tiles_per_seg, 0, 0))],
        out_specs=pl.BlockSpec((tn // LANE, tm, LANE), lambda i, j: (j, i, 0)),
        out_shape=jax.ShapeDtypeStruct((n_cols // LANE, T, LANE), BF16),
        compiler_params=_params("parallel", "arbitrary"),
        name="proj_heads",
    )(h, w, gains)


def _proj_flat_kernel(h_ref, w_ref, o_ref):
    o_ref[...] = jnp.dot(h_ref[...], w_ref[...], preferred_element_type=F32).astype(o_ref.dtype)


def _proj_gate_kernel(h_ref, w_ref, b_ref, o_ref):
    acc = jnp.dot(h_ref[...], w_ref[...], preferred_element_type=F32)
    o_ref[...] = jax.nn.sigmoid(acc + b_ref[...]).astype(o_ref.dtype)


def _proj_flat(h, w, layer, b=None, *, col0, n_cols):
    T, D = h.shape
    tm = _tile(T, 1024)
    tn = _tile(n_cols, 1024)
    assert col0 % tn == 0
    j0 = col0 // tn
    in_specs = [pl.BlockSpec((tm, D), lambda i, j: (i, 0)),
                pl.BlockSpec((None, D, tn), lambda i, j: (layer, 0, j + j0))]
    args = [h, w]
    if b is not None:
        in_specs.append(pl.BlockSpec((1, tn), lambda i, j: (0, j)))
        args.append(b.reshape(1, n_cols).astype(F32))
    return pl.pallas_call(
        _proj_flat_kernel if b is None else _proj_gate_kernel,
        grid=(T // tm, n_cols // tn),
        in_specs=in_specs,
        out_specs=pl.BlockSpec((tm, tn), lambda i, j: (i, j)),
        out_shape=jax.ShapeDtypeStruct((T, n_cols), BF16),
        compiler_params=_params("parallel", "arbitrary"),
        name="proj_flat" if b is None else "proj_gate",
    )(*args)


def _attn_a_bias_tile(rel_bias):
    H = rel_bias.shape[0]
    table = rel_bias.astype(F32)
    n_far = A_LEFT_CHUNKS * CHUNK - A_MAX_REL + CHUNK - 1
    far = jnp.broadcast_to(table[:, -1:], (H, n_far))
    near = table[:, :A_MAX_REL - CHUNK:-1]
    f = jnp.concatenate([far, near, jnp.zeros((H, 1), F32)], axis=1)
    L = f.shape[1] - 1
    windows = jnp.tile(f, (1, CHUNK))[:, :CHUNK * L].reshape(H, CHUNK, L)
    t = windows[:, :, CHUNK - 1:]
    masked = jnp.full((H, CHUNK, CHUNK), MASKED, F32)
    return jnp.concatenate([jnp.concatenate([t, masked], axis=2),
                            jnp.concatenate([masked, t], axis=2)], axis=1)


def _attn_a_kernel(q_ref, k_ref, v_ref, bias_ref, o_ref):
    S = q_ref.shape[0]
    for blk in range(S // A_QBLOCK):
        q0 = blk * A_QBLOCK
        k1 = q0 + A_QBLOCK
        k0 = max(0, k1 - A_BAND)
        b0 = A_BAND - (k1 - k0)
        s = lax.dot_general(q_ref[q0:k1, :], k_ref[k0:k1, :], (((1,), (1,)), ((), ())),
                            preferred_element_type=F32)
        s = s + bias_ref[:, b0:]
        e = jnp.exp(s - jnp.max(s, axis=-1, keepdims=True))
        l = jnp.sum(e, axis=-1, keepdims=True)
        o = jnp.dot(e.astype(BF16), v_ref[k0:k1, :], preferred_element_type=F32)
        o_ref[q0:k1, :] = (o / l).astype(o_ref.dtype)


def _attn_a(p_heads, bias_tile, *, batch, seq, heads):
    T = p_heads.shape[1]

    def spec(off):
        return pl.BlockSpec((None, seq, HEAD_DIM), lambda b, h: (h + off, b, 0))

    return pl.pallas_call(
        _attn_a_kernel,
        grid=(batch, heads),
        in_specs=[spec(0), spec(heads), spec(2 * heads),
                  pl.BlockSpec((None, A_QBLOCK, A_BAND), lambda b, h: (h, 0, 0))],
        out_specs=pl.BlockSpec((None, seq, HEAD_DIM), lambda b, h: (h, b, 0)),
        out_shape=jax.ShapeDtypeStruct((heads, T, HEAD_DIM), BF16),
        compiler_params=_params("parallel", "parallel"),
        name="attn_a",
    )(p_heads, p_heads, p_heads, bias_tile)


def _attn_b_kernel(lam_ref, q_ref, k_ref, v_ref, g_ref, o_ref, *, out_scale):
    S = q_ref.shape[1]
    lam = lam_ref[0]
    v = jnp.concatenate([v_ref[0], v_ref[1]], axis=-1)
    row_chunk = lax.broadcasted_iota(jnp.int32, (B_QBLOCK, B_QBLOCK), 0) // CHUNK
    col_chunk = lax.broadcasted_iota(jnp.int32, (B_QBLOCK, B_QBLOCK), 1) // CHUNK
    diag_mask = jnp.where(col_chunk <= row_chunk, 0.0, MASKED).astype(F32)
    nt = (((1,), (1,)), ((), ()))

    for blk in range(S // B_QBLOCK):
        q0 = blk * B_QBLOCK
        q1 = q0 + B_QBLOCK

        def softmax_parts(m):
            q = q_ref[m, q0:q1, :]
            sd = lax.dot_general(q, k_ref[m, q0:q1, :], nt, preferred_element_type=F32) + diag_mask
            mx = jnp.max(sd, axis=-1, keepdims=True)
            if blk:
                sl = lax.dot_general(q, k_ref[m, 0:q0, :], nt, preferred_element_type=F32)
                mx = jnp.maximum(mx, jnp.max(sl, axis=-1, keepdims=True))
                el = jnp.exp(sl - mx)
            ed = jnp.exp(sd - mx)
            l = jnp.sum(ed, axis=-1, keepdims=True)
            if blk:
                l = l + jnp.sum(el, axis=-1, keepdims=True)
                return el, ed, l
            return None, ed, l

        el1, ed1, l1 = softmax_parts(0)
        el2, ed2, l2 = softmax_parts(1)
        c1 = 1.0 / l1
        c2 = lam / l2
        pd = (ed1 * c1 - ed2 * c2).astype(BF16)
        ob = jnp.dot(pd, v[q0:q1, :], preferred_element_type=F32)
        if blk:
            pl_ = (el1 * c1 - el2 * c2).astype(BF16)
            ob = ob + jnp.dot(pl_, v[0:q0, :], preferred_element_type=F32)
        ms = jnp.mean(ob * ob, axis=-1, keepdims=True)
        o_ref[q0:q1, :] = (ob * lax.rsqrt(ms + EPS) * g_ref[...] * out_scale).astype(o_ref.dtype)


def _attn_b(p_heads, lam, subln_g, *, batch, seq, heads, q_grp, k_grp, v_grp, out_scale):
    T = p_heads.shape[1]

    def spec(grp):
        return pl.BlockSpec((2, seq, HEAD_DIM), lambda b, h: (h + grp // 2, b, 0))

    return pl.pallas_call(
        functools.partial(_attn_b_kernel, out_scale=out_scale),
        grid=(batch, heads),
        in_specs=[pl.BlockSpec(memory_space=pltpu.SMEM),
                  spec(q_grp), spec(k_grp), spec(v_grp),
                  pl.BlockSpec((1, 2 * HEAD_DIM), lambda b, h: (0, 0))],
        out_specs=pl.BlockSpec((None, seq, 2 * HEAD_DIM), lambda b, h: (h, b, 0)),
        out_shape=jax.ShapeDtypeStruct((heads, T, 2 * HEAD_DIM), BF16),
        compiler_params=_params("parallel", "parallel"),
        name="attn_b",
    )(lam, p_heads, p_heads, p_heads, subln_g.reshape(1, 2 * HEAD_DIM).astype(F32))


def _merge_out_kernel(x_ref, oa_ref, ob_ref, bg_ref, cg_ref, xc_ref, cgh_ref, xch_ref,
                      ga_ref, gb_ref, gc_ref, cw_ref, wo_ref, g2_ref, x1_ref, h2_ref, m_ref, *, seq):
    tm, D = x_ref.shape
    f = lambda r: r.astype(F32)
    starts_seq = (pl.program_id(0) * tm) % seq == 0
    row = lax.broadcasted_iota(jnp.int32, (tm, 1), 0)
    for g in range(D // LANE):
        sl = slice(g * LANE, (g + 1) * LANE)
        u = f(cg_ref[:, sl]) * f(xc_ref[:, sl])
        halo = jnp.where(starts_seq, 0.0, f(cgh_ref[:, sl]) * f(xch_ref[:, sl]))
        prev1 = halo[BF16_SUBLANES - 1:BF16_SUBLANES, :]
        prev2 = halo[BF16_SUBLANES - 2:BF16_SUBLANES - 1, :]
        u1 = jnp.where(row == 0, prev1, pltpu.roll(u, 1, 0))
        u2 = jnp.where(row == 0, prev2, jnp.where(row == 1, prev1, pltpu.roll(u, 2, 0)))
        conv = cw_ref[0:1, sl] * u2 + cw_ref[1:2, sl] * u1 + cw_ref[2:3, sl] * u
        half = slice((g % 2) * LANE, (g % 2 + 1) * LANE)
        m = (f(ga_ref[:, sl]) * f(oa_ref[g]) + f(gb_ref[:, sl]) * f(ob_ref[g // 2, :, half])
             + f(gc_ref[:, sl]) * (f(bg_ref[:, sl]) * conv))
        m_ref[:, sl] = m.astype(m_ref.dtype)
    x1 = x_ref[...] + jnp.dot(m_ref[...], wo_ref[...], preferred_element_type=F32)
    x1_ref[...] = x1
    ms = jnp.mean(x1 * x1, axis=-1, keepdims=True)
    h2_ref[...] = (x1 * lax.rsqrt(ms + EPS) * g2_ref[...]).astype(h2_ref.dtype)


def _merge_out(x, out_a, out_b, flat, gates, conv_w, w_o, layer, g2, *, seq):
    T, D = x.shape
    tm = _tile(seq, 256)
    halo_blocks = tm // BF16_SUBLANES

    def col(c):
        return pl.BlockSpec((tm, D), lambda i: (i, c))

    def halo(c):
        return pl.BlockSpec((BF16_SUBLANES, D), lambda i: (jnp.maximum(i * halo_blocks - 1, 0), c))

    return pl.pallas_call(
        functools.partial(_merge_out_kernel, seq=seq),
        grid=(T // tm,),
        in_specs=[pl.BlockSpec((tm, D), lambda i: (i, 0)),
                  pl.BlockSpec((out_a.shape[0], tm, HEAD_DIM), lambda i: (0, i, 0)),
                  pl.BlockSpec((out_b.shape[0], tm, 2 * HEAD_DIM), lambda i: (0, i, 0)),
                  col(0), col(1), col(2), halo(1), halo(2),
                  col(0), col(1), col(2),
                  pl.BlockSpec((3, D), lambda i: (0, 0)),
                  pl.BlockSpec((None, D, D), lambda i: (layer, 0, 0)),
                  pl.BlockSpec((1, D), lambda i: (0, 0))],
        out_specs=[pl.BlockSpec((tm, D), lambda i: (i, 0)),
                   pl.BlockSpec((tm, D), lambda i: (i, 0))],
        out_shape=[jax.ShapeDtypeStruct((T, D), F32), jax.ShapeDtypeStruct((T, D), BF16)],
        scratch_shapes=[pltpu.VMEM((tm, D), BF16)],
        compiler_params=_params("parallel"),
        name="merge_out",
    )(x, out_a, out_b, flat, flat, flat, flat, flat, gates, gates, gates,
      conv_w.astype(F32), w_o, g2.reshape(1, D).astype(F32))


def _mlp_kernel(x1_ref, h2_ref, w1_ref, w2_ref, gn_ref, x2_ref, *hn_ref, with_next):
    f = pl.program_id(1)
    a = jnp.dot(h2_ref[...], w1_ref[...], preferred_element_type=F32)
    a = jnp.square(jnp.maximum(a, 0.0)).astype(BF16)
    d = jnp.dot(a, w2_ref[...], preferred_element_type=F32)

    @pl.when(f == 0)
    def _():
        x2_ref[...] = x1_ref[...] + d

    @pl.when(f > 0)
    def _():
        x2_ref[...] += d

    if with_next:
        @pl.when(f == pl.num_programs(1) - 1)
        def _():
            x2 = x2_ref[...]
            ms = jnp.mean(x2 * x2, axis=-1, keepdims=True)
            hn_ref[0][...] = (x2 * lax.rsqrt(ms + EPS) * gn_ref[...]).astype(BF16)


def _mlp(x1, h2, w1, w2, layer, g_next):
    T, D = x1.shape
    FF = w1.shape[2]
    tm = _tile(T, 512)
    tf = _tile(FF, 1024)
    with_next = g_next is not None
    gn = (g_next if with_next else jnp.ones((D,), F32)).reshape(1, D).astype(F32)
    row = lambda i, f: (i, 0)
    out_specs = [pl.BlockSpec((tm, D), row)]
    out_shape = [jax.ShapeDtypeStruct((T, D), F32)]
    if with_next:
        out_specs.append(pl.BlockSpec((tm, D), row))
        out_shape.append(jax.ShapeDtypeStruct((T, D), BF16))
    out = pl.pallas_call(
        functools.partial(_mlp_kernel, with_next=with_next),
        grid=(T // tm, FF // tf),
        in_specs=[pl.BlockSpec((tm, D), row),
                  pl.BlockSpec((tm, D), row),
                  pl.BlockSpec((None, D, tf), lambda i, f: (layer, 0, f)),
                  pl.BlockSpec((None, tf, D), lambda i, f: (layer, f, 0)),
                  pl.BlockSpec((1, D), lambda i, f: (0, 0))],
        out_specs=out_specs,
        out_shape=out_shape,
        compiler_params=_params("parallel", "arbitrary"),
        name="mlp",
    )(x1, h2, w1, w2, gn)
    return (out[0], out[1]) if with_next else (out[0], None)


def kernel(x, norm1_g, w_in, w_gate, b_gate, a_qn_g, a_kn_g, a_rel_bias, b_qn_g, b_kn_g,
           b_lq1, b_lk1, b_lq2, b_lk2, b_subln_g, c_conv_w, w_o, norm2_g, w_mlp1, w_mlp2):
    B, S, D = x.shape
    depth = w_in.shape[0]
    T = B * S
    a_heads = D // HEAD_DIM
    b_heads = D // (2 * HEAD_DIM)
    assert S % B_QBLOCK == 0 and D % (2 * HEAD_DIM) == 0 and w_in.shape[2] == 9 * D
    scale = HEAD_DIM ** -0.5
    ones = jnp.ones((HEAD_DIM,), F32)

    xf = x.reshape(T, D).astype(F32)
    w_in_b, w_gate_b, w_o_b = w_in.astype(BF16), w_gate.astype(BF16), w_o.astype(BF16)
    w1_b, w2_b = w_mlp1.astype(BF16), w_mlp2.astype(BF16)

    h = _rmsnorm(xf, norm1_g[0])
    for l in range(depth):
        gains = jnp.stack([a_qn_g[l].astype(F32) * scale, a_kn_g[l].astype(F32), ones,
                           b_qn_g[l].astype(F32) * scale, b_kn_g[l].astype(F32), ones]).reshape(6, 1, HEAD_DIM)
        p_heads = _proj_heads(h, w_in_b, l, gains, n_cols=6 * D, seg_cols=D, plain_segs=(2, 5))
        flat = _proj_flat(h, w_in_b, l, col0=6 * D, n_cols=3 * D)
        gates = _proj_flat(h, w_gate_b, l, b_gate[l], col0=0, n_cols=3 * D)

        out_a = _attn_a(p_heads, _attn_a_bias_tile(a_rel_bias[l]), batch=B, seq=S, heads=a_heads)

        lambda_init = 0.8 - 0.6 * math.exp(-0.3 * l)
        lam = (jnp.exp(jnp.sum(b_lq1[l].astype(F32) * b_lk1[l].astype(F32)))
               - jnp.exp(jnp.sum(b_lq2[l].astype(F32) * b_lk2[l].astype(F32))) + lambda_init).reshape(1)
        out_b = _attn_b(p_heads, lam, b_subln_g[l], batch=B, seq=S, heads=b_heads,
                        q_grp=3 * a_heads, k_grp=4 * a_heads, v_grp=5 * a_heads,
                        out_scale=1.0 - lambda_init)

        x1, h2 = _merge_out(xf, out_a, out_b, flat, gates, c_conv_w[l], w_o_b, l, norm2_g[l], seq=S)
        xf, h = _mlp(x1, h2, w1_b, w2_b, l, norm1_g[l + 1] if l + 1 < depth else None)
    return xf.reshape(B, S, D).astype(x.dtype)
```

```python
import functools
import math

import jax
import jax.numpy as jnp
from jax import lax
from jax.experimental import pallas as pl
from jax.experimental.pallas import tpu as pltpu

F32 = jnp.float32
BF16 = jnp.bfloat16

LANE = 128
BF16_SUBLANES = 16
CHUNK = 64
HEAD_DIM = 128
EPS = 1e-6
A_LEFT_CHUNKS = 8
A_MAX_REL = 128
A_QBLOCK = 2 * CHUNK
A_BAND = (A_LEFT_CHUNKS + 2) * CHUNK
B_QBLOCK = 256
PROJ_ROW_CHUNK = 256
MERGE_ROW_CHUNK = 256
MASKED = -1e30
LOG2E = math.log2(math.e)
VMEM_LIMIT = 56 * 1024 * 1024


def _params(*sem):
    return pltpu.CompilerParams(dimension_semantics=sem, vmem_limit_bytes=VMEM_LIMIT)


def _tile(n, pref):
    if n <= pref:
        return n
    t = pref
    while n % t:
        t -= LANE
    assert t > 0
    return t


def _rmsnorm_kernel(x_ref, g_ref, o_ref):
    x = x_ref[...]
    ms = jnp.mean(x * x, axis=-1, keepdims=True)
    o_ref[...] = (x * lax.rsqrt(ms + EPS) * g_ref[...]).astype(o_ref.dtype)


def _rmsnorm(x, g):
    T, D = x.shape
    tm = _tile(T, 512)
    return pl.pallas_call(
        _rmsnorm_kernel,
        grid=(T // tm,),
        in_specs=[pl.BlockSpec((tm, D), lambda i: (i, 0)),
                  pl.BlockSpec((1, D), lambda i: (0, 0))],
        out_specs=pl.BlockSpec((tm, D), lambda i: (i, 0)),
        out_shape=jax.ShapeDtypeStruct((T, D), BF16),
        compiler_params=_params("parallel"),
        name="rmsnorm",
    )(x, g.reshape(1, D).astype(F32))


def _proj_heads_kernel(h_ref, w_ref, *rest, norm):
    o_ref = rest[-1]
    tm = h_ref.shape[0]
    rc = min(tm, PROJ_ROW_CHUNK)
    for r in range(tm // rc):
        rows = slice(r * rc, (r + 1) * rc)
        acc = jnp.dot(h_ref[rows, :], w_ref[...], preferred_element_type=F32)
        for g in range(o_ref.shape[0]):
            a = acc[:, g * LANE:(g + 1) * LANE]
            if norm:
                ms = jnp.mean(a * a, axis=-1, keepdims=True)
                a = a * lax.rsqrt(ms + EPS) * rest[0][...]
            o_ref[g, rows, :] = a.astype(o_ref.dtype)


def _proj_heads(h, w, layer, segs, gains=None, *, seg_cols):
    T, D = h.shape
    tm = _tile(T, 1024)
    tn = _tile(seg_cols, 1024)
    tps = seg_cols // tn
    n_cols = len(segs) * seg_cols

    def w_block(i, j):
        k = j // tps
        seg = sum(jnp.where(k == n, s, 0) for n, s in enumerate(segs))
        return (layer, 0, seg * tps + j % tps)

    in_specs = [pl.BlockSpec((tm, D), lambda i, j: (i, 0)),
                pl.BlockSpec((None, D, tn), w_block)]
    args = [h, w]
    if gains is not None:
        in_specs.append(pl.BlockSpec((None, 1, LANE), lambda i, j: (j // tps, 0, 0)))
        args.append(gains)
    return pl.pallas_call(
        functools.partial(_proj_heads_kernel, norm=gains is not None),
        grid=(T // tm, n_cols // tn),
        in_specs=in_specs,
        out_specs=pl.BlockSpec((tn // LANE, tm, LANE), lambda i, j: (j, i, 0)),
        out_shape=jax.ShapeDtypeStruct((n_cols // LANE, T, LANE), BF16),
        compiler_params=_params("parallel", "arbitrary"),
        name="proj_heads_norm" if gains is not None else "proj_heads_plain",
    )(*args)


def _proj_flat_kernel(h_ref, w_ref, o_ref):
    o_ref[...] = jnp.dot(h_ref[...], w_ref[...], preferred_element_type=F32).astype(o_ref.dtype)


def _proj_gate_kernel(h_ref, w_ref, b_ref, o_ref):
    tm = h_ref.shape[0]
    rc = min(tm, PROJ_ROW_CHUNK)
    for r in range(tm // rc):
        rows = slice(r * rc, (r + 1) * rc)
        acc = jnp.dot(h_ref[rows, :], w_ref[...], preferred_element_type=F32)
        o_ref[rows, :] = jax.nn.sigmoid(acc + b_ref[...]).astype(o_ref.dtype)


def _proj_flat(h, w, layer, b=None, *, col0, n_cols):
    T, D = h.shape
    tm = _tile(T, 1024)
    tn = _tile(n_cols, 1024)
    assert col0 % tn == 0
    j0 = col0 // tn
    in_specs = [pl.BlockSpec((tm, D), lambda i, j: (i, 0)),
                pl.BlockSpec((None, D, tn), lambda i, j: (layer, 0, j + j0))]
    args = [h, w]
    if b is not None:
        in_specs.append(pl.BlockSpec((1, tn), lambda i, j: (0, j)))
        args.append(b.reshape(1, n_cols).astype(F32))
    return pl.pallas_call(
        _proj_flat_kernel if b is None else _proj_gate_kernel,
        grid=(T // tm, n_cols // tn),
        in_specs=in_specs,
        out_specs=pl.BlockSpec((tm, tn), lambda i, j: (i, j)),
        out_shape=jax.ShapeDtypeStruct((T, n_cols), BF16),
        compiler_params=_params("parallel", "arbitrary"),
        name="proj_flat" if b is None else "proj_gate",
    )(*args)


def _attn_a_bias_tile(rel_bias):
    H = rel_bias.shape[0]
    table = rel_bias.astype(F32)
    n_far = A_LEFT_CHUNKS * CHUNK - A_MAX_REL + CHUNK - 1
    far = jnp.broadcast_to(table[:, -1:], (H, n_far))
    near = table[:, :A_MAX_REL - CHUNK:-1]
    f = jnp.concatenate([far, near, jnp.zeros((H, 1), F32)], axis=1)
    L = f.shape[1] - 1
    windows = jnp.tile(f, (1, CHUNK))[:, :CHUNK * L].reshape(H, CHUNK, L)
    t = windows[:, :, CHUNK - 1:]
    masked = jnp.full((H, CHUNK, CHUNK), MASKED, F32)
    t = t * LOG2E
    return jnp.concatenate([jnp.concatenate([t, masked], axis=2),
                            jnp.concatenate([masked, t], axis=2)], axis=1)


def _attn_a_kernel(q_ref, k_ref, v_ref, bias_ref, o_ref, vx_ref):
    S = q_ref.shape[0]
    n_blocks = S // A_QBLOCK
    vx_ref[:, :HEAD_DIM] = v_ref[...]
    vx_ref[:, HEAD_DIM:] = jnp.ones((S, HEAD_DIM), vx_ref.dtype)

    def band(blk):
        k1 = (blk + 1) * A_QBLOCK
        return max(0, k1 - A_BAND), k1

    def scores(blk):
        k0, k1 = band(blk)
        b0 = A_BAND - (k1 - k0)
        s = lax.dot_general(q_ref[blk * A_QBLOCK:k1, :], k_ref[k0:k1, :], (((1,), (1,)), ((), ())),
                            preferred_element_type=F32)
        return s + bias_ref[:, b0:]

    s_next = scores(0)
    for blk in range(n_blocks):
        s = s_next
        if blk + 1 < n_blocks:
            s_next = scores(blk + 1)
        k0, k1 = band(blk)
        e = jnp.exp2(s - jnp.max(s, axis=-1, keepdims=True))
        ox = jnp.dot(e.astype(BF16), vx_ref[k0:k1, :], preferred_element_type=F32)
        o_ref[blk * A_QBLOCK:k1, :] = (ox[:, :HEAD_DIM] / ox[:, HEAD_DIM:]).astype(o_ref.dtype)


def _attn_a(qk, vv, bias_tile, *, batch, seq, heads):
    T = qk.shape[1]

    def spec(off):
        return pl.BlockSpec((None, seq, HEAD_DIM), lambda b, h: (h + off, b, 0))

    return pl.pallas_call(
        _attn_a_kernel,
        grid=(batch, heads),
        in_specs=[spec(0), spec(heads), spec(0),
                  pl.BlockSpec((None, A_QBLOCK, A_BAND), lambda b, h: (h, 0, 0))],
        out_specs=pl.BlockSpec((None, seq, HEAD_DIM), lambda b, h: (h, b, 0)),
        out_shape=jax.ShapeDtypeStruct((heads, T, HEAD_DIM), BF16),
        scratch_shapes=[pltpu.VMEM((seq, 2 * HEAD_DIM), BF16)],
        compiler_params=_params("parallel", "parallel"),
        name="attn_a",
    )(qk, qk, vv, bias_tile)


def _attn_b_kernel(lam_ref, q_ref, k_ref, v_ref, g_ref, o_ref, *, out_scale):
    S = q_ref.shape[1]
    lam = lam_ref[0]
    v = jnp.concatenate([v_ref[0], v_ref[1]], axis=-1)
    row_chunk = lax.broadcasted_iota(jnp.int32, (B_QBLOCK, B_QBLOCK), 0) // CHUNK
    col_chunk = lax.broadcasted_iota(jnp.int32, (B_QBLOCK, B_QBLOCK), 1) // CHUNK
    diag_mask = jnp.where(col_chunk <= row_chunk, 0.0, MASKED).astype(F32)
    nt = (((1,), (1,)), ((), ()))

    def scores(blk):
        q0, q1 = blk * B_QBLOCK, (blk + 1) * B_QBLOCK
        out = []
        for m in range(2):
            q = q_ref[m, q0:q1, :]
            sd = lax.dot_general(q, k_ref[m, q0:q1, :], nt, preferred_element_type=F32) + diag_mask
            sl = lax.dot_general(q, k_ref[m, 0:q0, :], nt, preferred_element_type=F32) if blk else None
            out.append((sl, sd))
        return out

    def softmax_parts(sl, sd):
        mx = jnp.max(sd, axis=-1, keepdims=True)
        el = None
        if sl is not None:
            mx = jnp.maximum(mx, jnp.max(sl, axis=-1, keepdims=True))
            el = jnp.exp2(sl - mx)
        ed = jnp.exp2(sd - mx)
        l = jnp.sum(ed, axis=-1, keepdims=True)
        if sl is not None:
            l = l + jnp.sum(el, axis=-1, keepdims=True)
        return el, ed, l

    n_blocks = S // B_QBLOCK
    s_next = scores(0)
    for blk in range(n_blocks):
        q0, q1 = blk * B_QBLOCK, (blk + 1) * B_QBLOCK
        (sl1, sd1), (sl2, sd2) = s_next
        if blk + 1 < n_blocks:
            s_next = scores(blk + 1)
        el1, ed1, l1 = softmax_parts(sl1, sd1)
        el2, ed2, l2 = softmax_parts(sl2, sd2)
        r = lam * l1 / l2
        ob = jnp.dot((ed1 - ed2 * r).astype(BF16), v[q0:q1, :], preferred_element_type=F32)
        if blk:
            ob = ob + jnp.dot((el1 - el2 * r).astype(BF16), v[0:q0, :], preferred_element_type=F32)
        ob = ob / l1
        ms = jnp.mean(ob * ob, axis=-1, keepdims=True)
        o_ref[q0:q1, :] = (ob * lax.rsqrt(ms + EPS) * g_ref[...] * out_scale).astype(o_ref.dtype)


def _attn_b(qk, vv, lam, subln_g, *, batch, seq, heads, q_grp, k_grp, v_grp, out_scale):
    T = qk.shape[1]

    def spec(grp):
        return pl.BlockSpec((2, seq, HEAD_DIM), lambda b, h: (h + grp // 2, b, 0))

    return pl.pallas_call(
        functools.partial(_attn_b_kernel, out_scale=out_scale),
        grid=(batch, heads),
        in_specs=[pl.BlockSpec(memory_space=pltpu.SMEM),
                  spec(q_grp), spec(k_grp), spec(v_grp),
                  pl.BlockSpec((1, 2 * HEAD_DIM), lambda b, h: (0, 0))],
        out_specs=pl.BlockSpec((None, seq, 2 * HEAD_DIM), lambda b, h: (h, b, 0)),
        out_shape=jax.ShapeDtypeStruct((heads, T, 2 * HEAD_DIM), BF16),
        compiler_params=_params("parallel", "parallel"),
        name="attn_b",
    )(lam, qk, qk, vv, subln_g.reshape(1, 2 * HEAD_DIM).astype(F32))


def _merge_out_kernel(x_ref, oa_ref, ob_ref, bg_ref, cg_ref, xc_ref, cgh_ref, xch_ref,
                      ga_ref, gb_ref, gc_ref, cw_ref, wo_ref, g2_ref, x1_ref, h2_ref, m_ref, *, seq):
    tm, D = x_ref.shape
    f = lambda r: r.astype(F32)
    starts_seq = (pl.program_id(0) * tm) % seq == 0
    rc = min(tm, MERGE_ROW_CHUNK)
    row = lax.broadcasted_iota(jnp.int32, (rc, 1), 0)
    for r0 in range(0, tm, rc):
        rows = slice(r0, r0 + rc)
        before = slice(r0 - BF16_SUBLANES, r0)
        for g in range(D // LANE):
            sl = slice(g * LANE, (g + 1) * LANE)
            u = f(cg_ref[rows, sl]) * f(xc_ref[rows, sl])
            if r0:
                halo = f(cg_ref[before, sl]) * f(xc_ref[before, sl])
            else:
                halo = jnp.where(starts_seq, 0.0, f(cgh_ref[:, sl]) * f(xch_ref[:, sl]))
            prev1 = halo[BF16_SUBLANES - 1:BF16_SUBLANES, :]
            prev2 = halo[BF16_SUBLANES - 2:BF16_SUBLANES - 1, :]
            u1 = jnp.where(row == 0, prev1, pltpu.roll(u, 1, 0))
            u2 = jnp.where(row == 0, prev2, jnp.where(row == 1, prev1, pltpu.roll(u, 2, 0)))
            conv = cw_ref[0:1, sl] * u2 + cw_ref[1:2, sl] * u1 + cw_ref[2:3, sl] * u
            half = slice((g % 2) * LANE, (g % 2 + 1) * LANE)
            ab = ga_ref[rows, sl] * oa_ref[g, rows, :] + gb_ref[rows, sl] * ob_ref[g // 2, rows, half]
            m = f(ab) + f(gc_ref[rows, sl]) * (f(bg_ref[rows, sl]) * conv)
            m_ref[rows, sl] = m.astype(m_ref.dtype)
        x1 = x_ref[rows, :] + jnp.dot(m_ref[rows, :], wo_ref[...], preferred_element_type=F32)
        x1_ref[rows, :] = x1
        ms = jnp.mean(x1 * x1, axis=-1, keepdims=True)
        h2_ref[rows, :] = (x1 * lax.rsqrt(ms + EPS) * g2_ref[...]).astype(h2_ref.dtype)


def _merge_out(x, out_a, out_b, flat, gates, conv_w, w_o, layer, g2, *, seq):
    T, D = x.shape
    tm = _tile(seq, 256)
    halo_blocks = tm // BF16_SUBLANES

    def col(c):
        return pl.BlockSpec((tm, D), lambda i: (i, c))

    def halo(c):
        return pl.BlockSpec((BF16_SUBLANES, D), lambda i: (jnp.maximum(i * halo_blocks - 1, 0), c))

    return pl.pallas_call(
        functools.partial(_merge_out_kernel, seq=seq),
        grid=(T // tm,),
        in_specs=[pl.BlockSpec((tm, D), lambda i: (i, 0)),
                  pl.BlockSpec((out_a.shape[0], tm, HEAD_DIM), lambda i: (0, i, 0)),
                  pl.BlockSpec((out_b.shape[0], tm, 2 * HEAD_DIM), lambda i: (0, i, 0)),
                  col(0), col(1), col(2), halo(1), halo(2),
                  col(0), col(1), col(2),
                  pl.BlockSpec((3, D), lambda i: (0, 0)),
                  pl.BlockSpec((None, D, D), lambda i: (layer, 0, 0)),
                  pl.BlockSpec((1, D), lambda i: (0, 0))],
        out_specs=[pl.BlockSpec((tm, D), lambda i: (i, 0)),
                   pl.BlockSpec((tm, D), lambda i: (i, 0))],
        out_shape=[jax.ShapeDtypeStruct((T, D), F32), jax.ShapeDtypeStruct((T, D), BF16)],
        scratch_shapes=[pltpu.VMEM((tm, D), BF16)],
        compiler_params=_params("parallel"),
        name="merge_out",
    )(x, out_a, out_b, flat, flat, flat, flat, flat, gates, gates, gates,
      conv_w.astype(F32), w_o, g2.reshape(1, D).astype(F32))


def _mlp_kernel(x1_ref, h2_ref, w1_ref, w2_ref, gn_ref, x2_ref, *hn_ref, with_next):
    f = pl.program_id(1)
    tm = h2_ref.shape[0]
    rc = min(tm, PROJ_ROW_CHUNK)
    chunks = [slice(r * rc, (r + 1) * rc) for r in range(tm // rc)]

    @pl.when(f == 0)
    def _():
        x2_ref[...] = x1_ref[...]

    acts = [jnp.dot(h2_ref[rows, :], w1_ref[...], preferred_element_type=F32) for rows in chunks]
    for rows, a in zip(chunks, acts):
        a = jnp.square(jnp.maximum(a, 0.0)).astype(BF16)
        x2_ref[rows, :] += jnp.dot(a, w2_ref[...], preferred_element_type=F32)

    if with_next:
        @pl.when(f == pl.num_programs(1) - 1)
        def _():
            x2 = x2_ref[...]
            ms = jnp.mean(x2 * x2, axis=-1, keepdims=True)
            hn_ref[0][...] = (x2 * lax.rsqrt(ms + EPS) * gn_ref[...]).astype(BF16)


def _mlp(x1, h2, w1, w2, layer, g_next):
    T, D = x1.shape
    FF = w1.shape[2]
    tm = _tile(T, 512)
    tf = _tile(FF, 1024)
    with_next = g_next is not None
    gn = (g_next if with_next else jnp.ones((D,), F32)).reshape(1, D).astype(F32)
    row = lambda i, f: (i, 0)
    out_specs = [pl.BlockSpec((tm, D), row)]
    out_shape = [jax.ShapeDtypeStruct((T, D), F32)]
    if with_next:
        out_specs.append(pl.BlockSpec((tm, D), row))
        out_shape.append(jax.ShapeDtypeStruct((T, D), BF16))
    out = pl.pallas_call(
        functools.partial(_mlp_kernel, with_next=with_next),
        grid=(T // tm, FF // tf),
        in_specs=[pl.BlockSpec((tm, D), row),
                  pl.BlockSpec((tm, D), row),
                  pl.BlockSpec((None, D, tf), lambda i, f: (layer, 0, f)),
                  pl.BlockSpec((None, tf, D), lambda i, f: (layer, f, 0)),
                  pl.BlockSpec((1, D), lambda i, f: (0, 0))],
        out_specs=out_specs,
        out_shape=out_shape,
        compiler_params=_params("parallel", "arbitrary"),
        name="mlp",
    )(x1, h2, w1, w2, gn)
    return (out[0], out[1]) if with_next else (out[0], None)


def kernel(x, norm1_g, w_in, w_gate, b_gate, a_qn_g, a_kn_g, a_rel_bias, b_qn_g, b_kn_g,
           b_lq1, b_lk1, b_lq2, b_lk2, b_subln_g, c_conv_w, w_o, norm2_g, w_mlp1, w_mlp2):
    B, S, D = x.shape
    depth = w_in.shape[0]
    T = B * S
    a_heads = D // HEAD_DIM
    b_heads = D // (2 * HEAD_DIM)
    assert S % B_QBLOCK == 0 and D % (2 * HEAD_DIM) == 0 and w_in.shape[2] == 9 * D
    scale = HEAD_DIM ** -0.5 * LOG2E

    xf = x.reshape(T, D).astype(F32)
    w_in_b, w_gate_b, w_o_b = w_in.astype(BF16), w_gate.astype(BF16), w_o.astype(BF16)
    w1_b, w2_b = w_mlp1.astype(BF16), w_mlp2.astype(BF16)

    h = _rmsnorm(xf, norm1_g[0])
    for l in range(depth):
        gains = jnp.stack([a_qn_g[l].astype(F32) * scale, a_kn_g[l].astype(F32),
                           b_qn_g[l].astype(F32) * scale, b_kn_g[l].astype(F32)]).reshape(4, 1, HEAD_DIM)
        qk = _proj_heads(h, w_in_b, l, (0, 1, 3, 4), gains, seg_cols=D)
        vv = _proj_heads(h, w_in_b, l, (2, 5), seg_cols=D)
        flat = _proj_flat(h, w_in_b, l, col0=6 * D, n_cols=3 * D)
        gates = _proj_flat(h, w_gate_b, l, b_gate[l], col0=0, n_cols=3 * D)

        out_a = _attn_a(qk, vv, _attn_a_bias_tile(a_rel_bias[l]), batch=B, seq=S, heads=a_heads)

        lambda_init = 0.8 - 0.6 * math.exp(-0.3 * l)
        lam = (jnp.exp(jnp.sum(b_lq1[l].astype(F32) * b_lk1[l].astype(F32)))
               - jnp.exp(jnp.sum(b_lq2[l].astype(F32) * b_lk2[l].astype(F32))) + lambda_init).reshape(1)
        out_b = _attn_b(qk, vv, lam, b_subln_g[l], batch=B, seq=S, heads=b_heads,
                        q_grp=2 * a_heads, k_grp=3 * a_heads, v_grp=a_heads,
                        out_scale=1.0 - lambda_init)

        x1, h2 = _merge_out(xf, out_a, out_b, flat, gates, c_conv_w[l], w_o_b, l, norm2_g[l], seq=S)
        xf, h = _mlp(x1, h2, w1_b, w2_b, l, norm1_g[l + 1] if l + 1 < depth else None)
    return xf.reshape(B, S, D).astype(x.dtype)
```

```python
import functools
import math

import jax
import jax.numpy as jnp
from jax import lax
from jax.experimental import pallas as pl
from jax.experimental.pallas import tpu as pltpu

F32 = jnp.float32
BF16 = jnp.bfloat16

LANE = 128
CHUNK = 64
HEAD_DIM = 128
EPS = 1e-6
A_LEFT_CHUNKS = 8
A_MAX_REL = 128
A_QBLOCK = 2 * CHUNK
A_BAND = (A_LEFT_CHUNKS + 2) * CHUNK
B_QBLOCK = 256
PROJ_ROW_CHUNK = 256
CONV_COL_TILE = 256
MASKED = -1e30
LOG2E = math.log2(math.e)
VMEM_LIMIT = 56 * 1024 * 1024


def _params(*sem):
    return pltpu.CompilerParams(dimension_semantics=sem, vmem_limit_bytes=VMEM_LIMIT)


def _tile(n, pref):
    if n <= pref:
        return n
    t = pref
    while n % t:
        t -= LANE
    assert t > 0
    return t


def _rmsnorm_kernel(x_ref, g_ref, o_ref):
    x = x_ref[...]
    ms = jnp.mean(x * x, axis=-1, keepdims=True)
    o_ref[...] = (x * lax.rsqrt(ms + EPS) * g_ref[...]).astype(o_ref.dtype)


def _rmsnorm(x, g):
    T, D = x.shape
    tm = _tile(T, 512)
    return pl.pallas_call(
        _rmsnorm_kernel,
        grid=(T // tm,),
        in_specs=[pl.BlockSpec((tm, D), lambda i: (i, 0)),
                  pl.BlockSpec((1, D), lambda i: (0, 0))],
        out_specs=pl.BlockSpec((tm, D), lambda i: (i, 0)),
        out_shape=jax.ShapeDtypeStruct((T, D), BF16),
        compiler_params=_params("parallel"),
        name="rmsnorm",
    )(x, g.reshape(1, D).astype(F32))


def _cast_weights(w_refs, wb_ref):
    @pl.when(pl.program_id(1) == 0)
    def _():
        for n, w_ref in enumerate(w_refs):
            wb_ref[n] = w_ref[...].astype(wb_ref.dtype)


def _proj_heads_kernel(h_ref, w_ref, *rest, norm):
    o_ref, wb_ref = rest[-2:]
    _cast_weights([w_ref], wb_ref)
    tm = h_ref.shape[0]
    rc = min(tm, PROJ_ROW_CHUNK)
    for r in range(tm // rc):
        rows = slice(r * rc, (r + 1) * rc)
        acc = jnp.dot(h_ref[rows, :], wb_ref[0], preferred_element_type=F32)
        for g in range(o_ref.shape[0]):
            a = acc[:, g * LANE:(g + 1) * LANE]
            if norm:
                ms = jnp.mean(a * a, axis=-1, keepdims=True)
                a = a * lax.rsqrt(ms + EPS) * rest[0][...]
            o_ref[g, rows, :] = a.astype(o_ref.dtype)


def _proj_heads(h, w, layer, segs, gains=None, *, seg_cols):
    T, D = h.shape
    tm = _tile(T, 1024)
    tn = _tile(seg_cols, 1024)
    tps = seg_cols // tn
    n_cols = len(segs) * seg_cols

    def w_block(j, i):
        k = j // tps
        seg = sum(jnp.where(k == n, s, 0) for n, s in enumerate(segs))
        return (layer, 0, seg * tps + j % tps)

    in_specs = [pl.BlockSpec((tm, D), lambda j, i: (i, 0)),
                pl.BlockSpec((None, D, tn), w_block)]
    args = [h, w]
    if gains is not None:
        in_specs.append(pl.BlockSpec((None, 1, LANE), lambda j, i: (j // tps, 0, 0)))
        args.append(gains)
    return pl.pallas_call(
        functools.partial(_proj_heads_kernel, norm=gains is not None),
        grid=(n_cols // tn, T // tm),
        in_specs=in_specs,
        out_specs=pl.BlockSpec((tn // LANE, tm, LANE), lambda j, i: (j, i, 0)),
        out_shape=jax.ShapeDtypeStruct((n_cols // LANE, T, LANE), BF16),
        scratch_shapes=[pltpu.VMEM((1, D, tn), BF16)],
        compiler_params=_params("arbitrary", "arbitrary"),
        name="proj_heads_norm" if gains is not None else "proj_heads_plain",
    )(*args)


def _proj_gate_kernel(h_ref, w_ref, b_ref, o_ref, wb_ref):
    _cast_weights([w_ref], wb_ref)
    tm = h_ref.shape[0]
    rc = min(tm, PROJ_ROW_CHUNK)
    for r in range(tm // rc):
        rows = slice(r * rc, (r + 1) * rc)
        acc = jnp.dot(h_ref[rows, :], wb_ref[0], preferred_element_type=F32)
        o_ref[rows, :] = jax.nn.sigmoid(acc + b_ref[...]).astype(o_ref.dtype)


def _proj_gate(h, w, layer, b, *, n_cols):
    T, D = h.shape
    tm = _tile(T, 1024)
    tn = _tile(n_cols, 1024)
    return pl.pallas_call(
        _proj_gate_kernel,
        grid=(n_cols // tn, T // tm),
        in_specs=[pl.BlockSpec((tm, D), lambda j, i: (i, 0)),
                  pl.BlockSpec((None, D, tn), lambda j, i: (layer, 0, j)),
                  pl.BlockSpec((1, tn), lambda j, i: (0, j))],
        out_specs=pl.BlockSpec((tm, tn), lambda j, i: (i, j)),
        out_shape=jax.ShapeDtypeStruct((T, n_cols), BF16),
        scratch_shapes=[pltpu.VMEM((1, D, tn), BF16)],
        compiler_params=_params("arbitrary", "arbitrary"),
        name="proj_gate",
    )(h, w, b.reshape(1, -1).astype(F32))


def _proj_conv_kernel(h_ref, wbg_ref, wcg_ref, wxc_ref, wgc_ref, b_ref, cw_ref, o_ref, wb_ref):
    _cast_weights([wbg_ref, wcg_ref, wxc_ref, wgc_ref], wb_ref)
    S = h_ref.shape[0]
    rc = min(S, PROJ_ROW_CHUNK)
    row = lax.broadcasted_iota(jnp.int32, (rc, 1), 0)
    tail = jnp.zeros((8, o_ref.shape[1]), F32)
    for r0 in range(0, S, rc):
        rows = slice(r0, r0 + rc)
        hc = h_ref[rows, :]
        bg, cg, xc, gc = [jnp.dot(hc, wb_ref[n], preferred_element_type=F32) for n in range(4)]
        u = cg * xc
        prev1, prev2 = tail[7:8, :], tail[6:7, :]
        u1 = jnp.where(row == 0, prev1, pltpu.roll(u, 1, 0))
        u2 = jnp.where(row == 0, prev2, jnp.where(row == 1, prev1, pltpu.roll(u, 2, 0)))
        conv = cw_ref[0:1, :] * u2 + cw_ref[1:2, :] * u1 + cw_ref[2:3, :] * u
        o_ref[rows, :] = (jax.nn.sigmoid(gc + b_ref[...]) * (bg * conv)).astype(o_ref.dtype)
        tail = u[rc - 8:, :]


def _proj_conv(h, w_in, w_gate, layer, b_gate, conv_w, *, seq, bg_col, cg_col, xc_col, gc_col):
    T, D = h.shape
    tn = _tile(D, CONV_COL_TILE)

    def w_spec(col0):
        return pl.BlockSpec((None, D, tn), lambda j, i: (layer, 0, col0 // tn + j))

    return pl.pallas_call(
        _proj_conv_kernel,
        grid=(D // tn, T // seq),
        in_specs=[pl.BlockSpec((seq, D), lambda j, i: (i, 0)),
                  w_spec(bg_col), w_spec(cg_col), w_spec(xc_col), w_spec(gc_col),
                  pl.BlockSpec((1, tn), lambda j, i: (0, gc_col // tn + j)),
                  pl.BlockSpec((3, tn), lambda j, i: (0, j))],
        out_specs=pl.BlockSpec((seq, tn), lambda j, i: (i, j)),
        out_shape=jax.ShapeDtypeStruct((T, D), BF16),
        scratch_shapes=[pltpu.VMEM((4, D, tn), BF16)],
        compiler_params=_params("arbitrary", "arbitrary"),
        name="proj_conv",
    )(h, w_in, w_in, w_in, w_gate, b_gate.reshape(1, -1).astype(F32), conv_w.astype(F32))


def _attn_a_bias_tile(rel_bias):
    H = rel_bias.shape[0]
    table = rel_bias.astype(F32)
    n_far = A_LEFT_CHUNKS * CHUNK - A_MAX_REL + CHUNK - 1
    far = jnp.broadcast_to(table[:, -1:], (H, n_far))
    near = table[:, :A_MAX_REL - CHUNK:-1]
    f = jnp.concatenate([far, near, jnp.zeros((H, 1), F32)], axis=1)
    L = f.shape[1] - 1
    windows = jnp.tile(f, (1, CHUNK))[:, :CHUNK * L].reshape(H, CHUNK, L)
    t = windows[:, :, CHUNK - 1:]
    masked = jnp.full((H, CHUNK, CHUNK), MASKED, F32)
    t = t * LOG2E
    return jnp.concatenate([jnp.concatenate([t, masked], axis=2),
                            jnp.concatenate([masked, t], axis=2)], axis=1)


def _attn_a_kernel(q_ref, k_ref, v_ref, bias_ref, o_ref, vx_ref):
    S = q_ref.shape[0]
    n_blocks = S // A_QBLOCK
    vx_ref[:, :HEAD_DIM] = v_ref[...]
    vx_ref[:, HEAD_DIM:] = jnp.ones((S, HEAD_DIM), vx_ref.dtype)

    def band(blk):
        k1 = (blk + 1) * A_QBLOCK
        return max(0, k1 - A_BAND), k1

    def scores(blk):
        k0, k1 = band(blk)
        b0 = A_BAND - (k1 - k0)
        s = lax.dot_general(q_ref[blk * A_QBLOCK:k1, :], k_ref[k0:k1, :], (((1,), (1,)), ((), ())),
                            preferred_element_type=F32)
        return s + bias_ref[:, b0:]

    s_next = scores(0)
    for blk in range(n_blocks):
        s = s_next
        if blk + 1 < n_blocks:
            s_next = scores(blk + 1)
        k0, k1 = band(blk)
        e = jnp.exp2(s - jnp.max(s, axis=-1, keepdims=True))
        ox = jnp.dot(e.astype(BF16), vx_ref[k0:k1, :], preferred_element_type=F32)
        o_ref[blk * A_QBLOCK:k1, :] = (ox[:, :HEAD_DIM] / ox[:, HEAD_DIM:]).astype(o_ref.dtype)


def _attn_a(qk, vv, bias_tile, *, batch, seq, heads):
    T = qk.shape[1]

    def spec(off):
        return pl.BlockSpec((None, seq, HEAD_DIM), lambda b, h: (h + off, b, 0))

    return pl.pallas_call(
        _attn_a_kernel,
        grid=(batch, heads),
        in_specs=[spec(0), spec(heads), spec(0),
                  pl.BlockSpec((None, A_QBLOCK, A_BAND), lambda b, h: (h, 0, 0))],
        out_specs=pl.BlockSpec((None, seq, HEAD_DIM), lambda b, h: (h, b, 0)),
        out_shape=jax.ShapeDtypeStruct((heads, T, HEAD_DIM), BF16),
        scratch_shapes=[pltpu.VMEM((seq, 2 * HEAD_DIM), BF16)],
        compiler_params=_params("parallel", "parallel"),
        name="attn_a",
    )(qk, qk, vv, bias_tile)


def _attn_b_kernel(lam_ref, q_ref, k_ref, v_ref, g_ref, o_ref, *, out_scale):
    S = q_ref.shape[1]
    lam = lam_ref[0]
    v = jnp.concatenate([v_ref[0], v_ref[1]], axis=-1)
    row_chunk = lax.broadcasted_iota(jnp.int32, (B_QBLOCK, B_QBLOCK), 0) // CHUNK
    col_chunk = lax.broadcasted_iota(jnp.int32, (B_QBLOCK, B_QBLOCK), 1) // CHUNK
    diag_mask = jnp.where(col_chunk <= row_chunk, 0.0, MASKED).astype(F32)
    nt = (((1,), (1,)), ((), ()))

    def scores(blk):
        q0, q1 = blk * B_QBLOCK, (blk + 1) * B_QBLOCK
        out = []
        for m in range(2):
            q = q_ref[m, q0:q1, :]
            sd = lax.dot_general(q, k_ref[m, q0:q1, :], nt, preferred_element_type=F32) + diag_mask
            sl = lax.dot_general(q, k_ref[m, 0:q0, :], nt, preferred_element_type=F32) if blk else None
            out.append((sl, sd))
        return out

    def softmax_parts(sl, sd):
        mx = jnp.max(sd, axis=-1, keepdims=True)
        el = None
        if sl is not None:
            mx = jnp.maximum(mx, jnp.max(sl, axis=-1, keepdims=True))
            el = jnp.exp2(sl - mx)
        ed = jnp.exp2(sd - mx)
        l = jnp.sum(ed, axis=-1, keepdims=True)
        if sl is not None:
            l = l + jnp.sum(el, axis=-1, keepdims=True)
        return el, ed, l

    n_blocks = S // B_QBLOCK
    s_next = scores(0)
    for blk in range(n_blocks):
        q0, q1 = blk * B_QBLOCK, (blk + 1) * B_QBLOCK
        (sl1, sd1), (sl2, sd2) = s_next
        if blk + 1 < n_blocks:
            s_next = scores(blk + 1)
        el1, ed1, l1 = softmax_parts(sl1, sd1)
        el2, ed2, l2 = softmax_parts(sl2, sd2)
        r = lam * l1 / l2
        ob = jnp.dot((ed1 - ed2 * r).astype(BF16), v[q0:q1, :], preferred_element_type=F32)
        if blk:
            ob = ob + jnp.dot((el1 - el2 * r).astype(BF16), v[0:q0, :], preferred_element_type=F32)
        ob = ob / l1
        ms = jnp.mean(ob * ob, axis=-1, keepdims=True)
        o_ref[q0:q1, :] = (ob * lax.rsqrt(ms + EPS) * g_ref[...] * out_scale).astype(o_ref.dtype)


def _attn_b(qk, vv, lam, subln_g, *, batch, seq, heads, q_grp, k_grp, v_grp, out_scale):
    T = qk.shape[1]

    def spec(grp):
        return pl.BlockSpec((2, seq, HEAD_DIM), lambda b, h: (h + grp // 2, b, 0))

    return pl.pallas_call(
        functools.partial(_attn_b_kernel, out_scale=out_scale),
        grid=(batch, heads),
        in_specs=[pl.BlockSpec(memory_space=pltpu.SMEM),
                  spec(q_grp), spec(k_grp), spec(v_grp),
                  pl.BlockSpec((1, 2 * HEAD_DIM), lambda b, h: (0, 0))],
        out_specs=pl.BlockSpec((None, seq, 2 * HEAD_DIM), lambda b, h: (h, b, 0)),
        out_shape=jax.ShapeDtypeStruct((heads, T, 2 * HEAD_DIM), BF16),
        compiler_params=_params("parallel", "parallel"),
        name="attn_b",
    )(lam, qk, qk, vv, subln_g.reshape(1, 2 * HEAD_DIM).astype(F32))


def _merge_out_kernel(x_ref, oa_ref, ob_ref, oc_ref, ga_ref, gb_ref, wo_ref, g2_ref, x1_ref, h2_ref, m_ref):
    D = x_ref.shape[1]
    for g in range(D // LANE):
        sl = slice(g * LANE, (g + 1) * LANE)
        half = slice((g % 2) * LANE, (g % 2 + 1) * LANE)
        m_ref[:, sl] = ga_ref[:, sl] * oa_ref[g] + gb_ref[:, sl] * ob_ref[g // 2, :, half] + oc_ref[:, sl]
    x1 = x_ref[...] + jnp.dot(m_ref[...], wo_ref[...], preferred_element_type=F32)
    x1_ref[...] = x1
    ms = jnp.mean(x1 * x1, axis=-1, keepdims=True)
    h2_ref[...] = (x1 * lax.rsqrt(ms + EPS) * g2_ref[...]).astype(h2_ref.dtype)


def _merge_out(x, out_a, out_b, out_c, gates, w_o, layer, g2):
    T, D = x.shape
    tm = _tile(T, 256)
    row = lambda i: (i, 0)
    return pl.pallas_call(
        _merge_out_kernel,
        grid=(T // tm,),
        in_specs=[pl.BlockSpec((tm, D), row),
                  pl.BlockSpec((out_a.shape[0], tm, HEAD_DIM), lambda i: (0, i, 0)),
                  pl.BlockSpec((out_b.shape[0], tm, 2 * HEAD_DIM), lambda i: (0, i, 0)),
                  pl.BlockSpec((tm, D), row),
                  pl.BlockSpec((tm, D), row), pl.BlockSpec((tm, D), lambda i: (i, 1)),
                  pl.BlockSpec((None, D, D), lambda i: (layer, 0, 0)),
                  pl.BlockSpec((1, D), lambda i: (0, 0))],
        out_specs=[pl.BlockSpec((tm, D), row), pl.BlockSpec((tm, D), row)],
        out_shape=[jax.ShapeDtypeStruct((T, D), F32), jax.ShapeDtypeStruct((T, D), BF16)],
        scratch_shapes=[pltpu.VMEM((tm, D), BF16)],
        compiler_params=_params("parallel"),
        name="merge_out",
    )(x, out_a, out_b, out_c, gates, gates, w_o, g2.reshape(1, D).astype(F32))


def _mlp_kernel(x1_ref, h2_ref, w1_ref, w2_ref, gn_ref, x2_ref, *hn_ref, with_next):
    f = pl.program_id(1)
    tm = h2_ref.shape[0]
    rc = min(tm, PROJ_ROW_CHUNK)
    chunks = [slice(r * rc, (r + 1) * rc) for r in range(tm // rc)]

    @pl.when(f == 0)
    def _():
        x2_ref[...] = x1_ref[...]

    acts = [jnp.dot(h2_ref[rows, :], w1_ref[...], preferred_element_type=F32) for rows in chunks]
    for rows, a in zip(chunks, acts):
        a = jnp.square(jnp.maximum(a, 0.0)).astype(BF16)
        x2_ref[rows, :] += jnp.dot(a, w2_ref[...], preferred_element_type=F32)

    if with_next:
        @pl.when(f == pl.num_programs(1) - 1)
        def _():
            x2 = x2_ref[...]
            ms = jnp.mean(x2 * x2, axis=-1, keepdims=True)
            hn_ref[0][...] = (x2 * lax.rsqrt(ms + EPS) * gn_ref[...]).astype(BF16)


def _mlp(x1, h2, w1, w2, layer, g_next):
    T, D = x1.shape
    FF = w1.shape[2]
    tm = _tile(T, 512)
    tf = _tile(FF, 1024)
    with_next = g_next is not None
    gn = (g_next if with_next else jnp.ones((D,), F32)).reshape(1, D).astype(F32)
    row = lambda i, f: (i, 0)
    out_specs = [pl.BlockSpec((tm, D), row)]
    out_shape = [jax.ShapeDtypeStruct((T, D), F32)]
    if with_next:
        out_specs.append(pl.BlockSpec((tm, D), row))
        out_shape.append(jax.ShapeDtypeStruct((T, D), BF16))
    out = pl.pallas_call(
        functools.partial(_mlp_kernel, with_next=with_next),
        grid=(T // tm, FF // tf),
        in_specs=[pl.BlockSpec((tm, D), row),
                  pl.BlockSpec((tm, D), row),
                  pl.BlockSpec((None, D, tf), lambda i, f: (layer, 0, f)),
                  pl.BlockSpec((None, tf, D), lambda i, f: (layer, f, 0)),
                  pl.BlockSpec((1, D), lambda i, f: (0, 0))],
        out_specs=out_specs,
        out_shape=out_shape,
        compiler_params=_params("parallel", "arbitrary"),
        name="mlp",
    )(x1, h2, w1, w2, gn)
    return (out[0], out[1]) if with_next else (out[0], None)


def kernel(x, norm1_g, w_in, w_gate, b_gate, a_qn_g, a_kn_g, a_rel_bias, b_qn_g, b_kn_g,
           b_lq1, b_lk1, b_lq2, b_lk2, b_subln_g, c_conv_w, w_o, norm2_g, w_mlp1, w_mlp2):
    B, S, D = x.shape
    depth = w_in.shape[0]
    T = B * S
    a_heads = D // HEAD_DIM
    b_heads = D // (2 * HEAD_DIM)
    assert S % B_QBLOCK == 0 and D % (2 * HEAD_DIM) == 0 and w_in.shape[2] == 9 * D
    scale = HEAD_DIM ** -0.5 * LOG2E

    xf = x.reshape(T, D).astype(F32)
    w_in, w_gate = w_in.astype(F32), w_gate.astype(F32)
    w_o_b, w1_b, w2_b = w_o.astype(BF16), w_mlp1.astype(BF16), w_mlp2.astype(BF16)

    h = _rmsnorm(xf, norm1_g[0])
    for l in range(depth):
        gains = jnp.stack([a_qn_g[l].astype(F32) * scale, a_kn_g[l].astype(F32),
                           b_qn_g[l].astype(F32) * scale, b_kn_g[l].astype(F32)]).reshape(4, 1, HEAD_DIM)
        qk = _proj_heads(h, w_in, l, (0, 1, 3, 4), gains, seg_cols=D)
        vv = _proj_heads(h, w_in, l, (2, 5), seg_cols=D)
        gates = _proj_gate(h, w_gate, l, b_gate[l], n_cols=2 * D)
        out_c = _proj_conv(h, w_in, w_gate, l, b_gate[l], c_conv_w[l], seq=S,
                           bg_col=6 * D, cg_col=7 * D, xc_col=8 * D, gc_col=2 * D)

        out_a = _attn_a(qk, vv, _attn_a_bias_tile(a_rel_bias[l]), batch=B, seq=S, heads=a_heads)

        lambda_init = 0.8 - 0.6 * math.exp(-0.3 * l)
        lam = (jnp.exp(jnp.sum(b_lq1[l].astype(F32) * b_lk1[l].astype(F32)))
               - jnp.exp(jnp.sum(b_lq2[l].astype(F32) * b_lk2[l].astype(F32))) + lambda_init).reshape(1)
        out_b = _attn_b(qk, vv, lam, b_subln_g[l], batch=B, seq=S, heads=b_heads,
                        q_grp=2 * a_heads, k_grp=3 * a_heads, v_grp=a_heads,
                        out_scale=1.0 - lambda_init)

        x1, h2 = _merge_out(xf, out_a, out_b, out_c, gates, w_o_b, l, norm2_g[l])
        xf, h = _mlp(x1, h2, w1_b, w2_b, l, norm1_g[l + 1] if l + 1 < depth else None)
    return xf.reshape(B, S, D).astype(x.dtype)
```

```python
import functools
import math

import jax
import jax.numpy as jnp
from jax import lax
from jax.experimental import pallas as pl
from jax.experimental.pallas import tpu as pltpu

F32 = jnp.float32
BF16 = jnp.bfloat16

LANE = 128
CHUNK = 64
HEAD_DIM = 128
EPS = 1e-6
A_LEFT_CHUNKS = 8
A_MAX_REL = 128
A_QBLOCK = 2 * CHUNK
A_BAND = (A_LEFT_CHUNKS + 2) * CHUNK
B_QBLOCK = 256
PROJ_ROW_CHUNK = 256
CONV_COL_TILE = 256
MASKED = -1e30
LOG2E = math.log2(math.e)
VMEM_LIMIT = 56 * 1024 * 1024


def _params(*sem):
    return pltpu.CompilerParams(dimension_semantics=sem, vmem_limit_bytes=VMEM_LIMIT)


def _tile(n, pref):
    if n <= pref:
        return n
    t = pref
    while n % t:
        t -= LANE
    assert t > 0
    return t


def _rmsnorm_kernel(x_ref, g_ref, o_ref):
    x = x_ref[...]
    ms = jnp.mean(x * x, axis=-1, keepdims=True)
    o_ref[...] = (x * lax.rsqrt(ms + EPS) * g_ref[...]).astype(o_ref.dtype)


def _rmsnorm(x, g):
    T, D = x.shape
    tm = _tile(T, 512)
    return pl.pallas_call(
        _rmsnorm_kernel,
        grid=(T // tm,),
        in_specs=[pl.BlockSpec((tm, D), lambda i: (i, 0)),
                  pl.BlockSpec((1, D), lambda i: (0, 0))],
        out_specs=pl.BlockSpec((tm, D), lambda i: (i, 0)),
        out_shape=jax.ShapeDtypeStruct((T, D), BF16),
        compiler_params=_params("parallel"),
        name="rmsnorm",
    )(x, g.reshape(1, D).astype(F32))


def _cast_weights(w_refs, wb_ref):
    @pl.when(pl.program_id(1) == 0)
    def _():
        for n, w_ref in enumerate(w_refs):
            wb_ref[n] = w_ref[...].astype(wb_ref.dtype)


def _proj_heads_kernel(h_ref, w_ref, *rest, epilogue):
    p_ref, (o_ref, wb_ref) = rest[0], rest[-2:]
    _cast_weights([w_ref], wb_ref)
    tm = h_ref.shape[0]
    rc = min(tm, PROJ_ROW_CHUNK)
    for r in range(tm // rc):
        rows = slice(r * rc, (r + 1) * rc)
        acc = jnp.dot(h_ref[rows, :], wb_ref[0], preferred_element_type=F32)
        for g in range(o_ref.shape[0]):
            cols = slice(g * LANE, (g + 1) * LANE)
            a = acc[:, cols]
            if epilogue == "norm":
                ms = jnp.mean(a * a, axis=-1, keepdims=True)
                a = a * lax.rsqrt(ms + EPS) * p_ref[...]
            elif epilogue == "gate":
                a = jax.nn.sigmoid(a + p_ref[:, cols])
            o_ref[g, rows, :] = a.astype(o_ref.dtype)


def _proj_heads(h, w, layer, segs, *, seg_cols, gains=None, bias=None):
    T, D = h.shape
    tm = _tile(T, 1024)
    tn = _tile(seg_cols, 1024)
    tps = seg_cols // tn
    n_cols = len(segs) * seg_cols

    def col_block(j):
        k = j // tps
        seg = sum(jnp.where(k == n, s, 0) for n, s in enumerate(segs))
        return seg * tps + j % tps

    if gains is not None:
        epilogue, p, p_spec = "norm", gains, pl.BlockSpec((None, 1, LANE), lambda j, i: (j // tps, 0, 0))
    elif bias is not None:
        epilogue, p, p_spec = "gate", bias, pl.BlockSpec((1, tn), lambda j, i: (0, col_block(j)))
    else:
        epilogue, p, p_spec = "plain", None, None
    return pl.pallas_call(
        functools.partial(_proj_heads_kernel, epilogue=epilogue),
        grid=(n_cols // tn, T // tm),
        in_specs=[pl.BlockSpec((tm, D), lambda j, i: (i, 0)),
                  pl.BlockSpec((None, D, tn), lambda j, i: (layer, 0, col_block(j)))]
                 + ([] if p is None else [p_spec]),
        out_specs=pl.BlockSpec((tn // LANE, tm, LANE), lambda j, i: (j, i, 0)),
        out_shape=jax.ShapeDtypeStruct((n_cols // LANE, T, LANE), BF16),
        scratch_shapes=[pltpu.VMEM((1, D, tn), BF16)],
        compiler_params=_params("arbitrary", "arbitrary"),
        name="proj_heads_" + epilogue,
    )(h, w, *([] if p is None else [p]))


def _proj_conv_kernel(h_ref, wbg_ref, wcg_ref, wxc_ref, wgc_ref, b_ref, cw_ref, o_ref, wb_ref):
    _cast_weights([wbg_ref, wcg_ref, wxc_ref, wgc_ref], wb_ref)
    S = h_ref.shape[0]
    rc = min(S, PROJ_ROW_CHUNK)
    row = lax.broadcasted_iota(jnp.int32, (rc, 1), 0)
    tail = jnp.zeros((8, o_ref.shape[1]), F32)
    for r0 in range(0, S, rc):
        rows = slice(r0, r0 + rc)
        hc = h_ref[rows, :]
        bg, cg, xc, gc = [jnp.dot(hc, wb_ref[n], preferred_element_type=F32) for n in range(4)]
        u = cg * xc
        prev1, prev2 = tail[7:8, :], tail[6:7, :]
        u1 = jnp.where(row == 0, prev1, pltpu.roll(u, 1, 0))
        u2 = jnp.where(row == 0, prev2, jnp.where(row == 1, prev1, pltpu.roll(u, 2, 0)))
        conv = cw_ref[0:1, :] * u2 + cw_ref[1:2, :] * u1 + cw_ref[2:3, :] * u
        o_ref[rows, :] = (jax.nn.sigmoid(gc + b_ref[...]) * (bg * conv)).astype(o_ref.dtype)
        tail = u[rc - 8:, :]


def _proj_conv(h, w_in, w_gate, layer, b_gate, conv_w, *, seq, bg_col, cg_col, xc_col, gc_col):
    T, D = h.shape
    tn = _tile(D, CONV_COL_TILE)

    def w_spec(col0):
        return pl.BlockSpec((None, D, tn), lambda j, i: (layer, 0, col0 // tn + j))

    return pl.pallas_call(
        _proj_conv_kernel,
        grid=(D // tn, T // seq),
        in_specs=[pl.BlockSpec((seq, D), lambda j, i: (i, 0)),
                  w_spec(bg_col), w_spec(cg_col), w_spec(xc_col), w_spec(gc_col),
                  pl.BlockSpec((1, tn), lambda j, i: (0, gc_col // tn + j)),
                  pl.BlockSpec((3, tn), lambda j, i: (0, j))],
        out_specs=pl.BlockSpec((seq, tn), lambda j, i: (i, j)),
        out_shape=jax.ShapeDtypeStruct((T, D), BF16),
        scratch_shapes=[pltpu.VMEM((4, D, tn), BF16)],
        compiler_params=_params("arbitrary", "arbitrary"),
        name="proj_conv",
    )(h, w_in, w_in, w_in, w_gate, b_gate.reshape(1, -1).astype(F32), conv_w.astype(F32))


def _attn_a_bias_tile(rel_bias):
    H = rel_bias.shape[0]
    table = rel_bias.astype(F32)
    n_far = A_LEFT_CHUNKS * CHUNK - A_MAX_REL + CHUNK - 1
    far = jnp.broadcast_to(table[:, -1:], (H, n_far))
    near = table[:, :A_MAX_REL - CHUNK:-1]
    f = jnp.concatenate([far, near, jnp.zeros((H, 1), F32)], axis=1)
    L = f.shape[1] - 1
    windows = jnp.tile(f, (1, CHUNK))[:, :CHUNK * L].reshape(H, CHUNK, L)
    t = windows[:, :, CHUNK - 1:]
    masked = jnp.full((H, CHUNK, CHUNK), MASKED, F32)
    t = t * LOG2E
    return jnp.concatenate([jnp.concatenate([t, masked], axis=2),
                            jnp.concatenate([masked, t], axis=2)], axis=1)


def _attn_a_steps(q_ref, k_ref, v_ref, bias_ref, gate_ref, o_ref, vx_ref):
    S = q_ref.shape[0]
    n_blocks = S // A_QBLOCK
    vx_ref[:, :HEAD_DIM] = v_ref[...]
    vx_ref[:, HEAD_DIM:] = jnp.ones((S, HEAD_DIM), vx_ref.dtype)

    def band(blk):
        k1 = (blk + 1) * A_QBLOCK
        return max(0, k1 - A_BAND), k1

    def scores(blk):
        k0, k1 = band(blk)
        b0 = A_BAND - (k1 - k0)
        s = lax.dot_general(q_ref[blk * A_QBLOCK:k1, :], k_ref[k0:k1, :], (((1,), (1,)), ((), ())),
                            preferred_element_type=F32)
        return s + bias_ref[:, b0:]

    s_next = scores(0)
    for blk in range(n_blocks):
        s = s_next
        if blk + 1 < n_blocks:
            s_next = scores(blk + 1)
        k0, k1 = band(blk)
        e = jnp.exp2(s - jnp.max(s, axis=-1, keepdims=True))
        ox = jnp.dot(e.astype(BF16), vx_ref[k0:k1, :], preferred_element_type=F32)
        rows = slice(blk * A_QBLOCK, k1)
        o_ref[rows, :] = (ox[:, :HEAD_DIM] / ox[:, HEAD_DIM:]).astype(o_ref.dtype) * gate_ref[rows, :]
        yield


def _attn_b_steps(lam_ref, q_ref, k_ref, v_ref, g_ref, gate_ref, o_ref, *, out_scale):
    S = q_ref.shape[1]
    lam = lam_ref[0]
    v = jnp.concatenate([v_ref[0], v_ref[1]], axis=-1)
    row_chunk = lax.broadcasted_iota(jnp.int32, (B_QBLOCK, B_QBLOCK), 0) // CHUNK
    col_chunk = lax.broadcasted_iota(jnp.int32, (B_QBLOCK, B_QBLOCK), 1) // CHUNK
    diag_mask = jnp.where(col_chunk <= row_chunk, 0.0, MASKED).astype(F32)
    nt = (((1,), (1,)), ((), ()))

    def scores(blk):
        q0, q1 = blk * B_QBLOCK, (blk + 1) * B_QBLOCK
        out = []
        for m in range(2):
            q = q_ref[m, q0:q1, :]
            sd = lax.dot_general(q, k_ref[m, q0:q1, :], nt, preferred_element_type=F32) + diag_mask
            sl = lax.dot_general(q, k_ref[m, 0:q0, :], nt, preferred_element_type=F32) if blk else None
            out.append((sl, sd))
        return out

    def softmax_parts(sl, sd):
        mx = jnp.max(sd, axis=-1, keepdims=True)
        el = None
        if sl is not None:
            mx = jnp.maximum(mx, jnp.max(sl, axis=-1, keepdims=True))
            el = jnp.exp2(sl - mx)
        ed = jnp.exp2(sd - mx)
        l = jnp.sum(ed, axis=-1, keepdims=True)
        if sl is not None:
            l = l + jnp.sum(el, axis=-1, keepdims=True)
        return el, ed, l

    n_blocks = S // B_QBLOCK
    s_next = scores(0)
    for blk in range(n_blocks):
        q0, q1 = blk * B_QBLOCK, (blk + 1) * B_QBLOCK
        (sl1, sd1), (sl2, sd2) = s_next
        if blk + 1 < n_blocks:
            s_next = scores(blk + 1)
        yield
        el1, ed1, l1 = softmax_parts(sl1, sd1)
        el2, ed2, l2 = softmax_parts(sl2, sd2)
        r = lam * l1 / l2
        ob = jnp.dot((ed1 - ed2 * r).astype(BF16), v[q0:q1, :], preferred_element_type=F32)
        if blk:
            ob = ob + jnp.dot((el1 - el2 * r).astype(BF16), v[0:q0, :], preferred_element_type=F32)
        ob = ob / l1
        ms = jnp.mean(ob * ob, axis=-1, keepdims=True)
        y = (ob * lax.rsqrt(ms + EPS) * g_ref[...] * out_scale).astype(o_ref.dtype)
        for half in range(2):
            cols = slice(half * HEAD_DIM, (half + 1) * HEAD_DIM)
            o_ref[q0:q1, cols] = y[:, cols] * gate_ref[half, q0:q1, :]
        yield


def _attn_kernel(lam_ref, qa_ref, ka_ref, va_ref, bias_ref, ga_ref, qb_ref, kb_ref, vb_ref, g_ref, gb_ref,
                 oa_ref, ob_ref, vx_ref, *, out_scale):
    a_progs = [_attn_a_steps(qa_ref.at[i], ka_ref.at[i], va_ref.at[i], bias_ref.at[i], ga_ref.at[i],
                             oa_ref.at[i], vx_ref.at[i]) for i in range(qa_ref.shape[0])]
    b_prog = _attn_b_steps(lam_ref, qb_ref, kb_ref, vb_ref, g_ref, gb_ref, ob_ref, out_scale=out_scale)
    S = qa_ref.shape[1]
    a_per_b = (S // A_QBLOCK) // (S // B_QBLOCK)
    for _ in range(S // B_QBLOCK):
        next(b_prog)
        for _ in range(a_per_b):
            for prog in a_progs:
                next(prog)
        next(b_prog)


def _attention(qk, vv, gates, bias_tile, lam, subln_g, *, batch, seq, b_heads, out_scale):
    T = qk.shape[1]

    def pair(seg):
        return pl.BlockSpec((2, seq, HEAD_DIM), lambda b, h: (seg * b_heads + h, b, 0))

    return pl.pallas_call(
        functools.partial(_attn_kernel, out_scale=out_scale),
        grid=(batch, b_heads),
        in_specs=[pl.BlockSpec(memory_space=pltpu.SMEM),
                  pair(0), pair(1), pair(0),
                  pl.BlockSpec((2, A_QBLOCK, A_BAND), lambda b, h: (h, 0, 0)), pair(0),
                  pair(2), pair(3), pair(1),
                  pl.BlockSpec((1, 2 * HEAD_DIM), lambda b, h: (0, 0)), pair(1)],
        out_specs=[pl.BlockSpec((2, seq, HEAD_DIM), lambda b, h: (h, b, 0)),
                   pl.BlockSpec((None, seq, 2 * HEAD_DIM), lambda b, h: (h, b, 0))],
        out_shape=[jax.ShapeDtypeStruct((2 * b_heads, T, HEAD_DIM), BF16),
                   jax.ShapeDtypeStruct((b_heads, T, 2 * HEAD_DIM), BF16)],
        scratch_shapes=[pltpu.VMEM((2, seq, 2 * HEAD_DIM), BF16)],
        compiler_params=_params("parallel", "parallel"),
        name="attention",
    )(lam, qk, qk, vv, bias_tile, gates, qk, qk, vv, subln_g.reshape(1, 2 * HEAD_DIM).astype(F32), gates)


def _merge_out_kernel(x_ref, oa_ref, ob_ref, oc_ref, wo_ref, g2_ref, x1_ref, h2_ref, m_ref):
    D = x_ref.shape[1]
    for g in range(D // LANE):
        sl = slice(g * LANE, (g + 1) * LANE)
        half = slice((g % 2) * LANE, (g % 2 + 1) * LANE)
        m_ref[:, sl] = oa_ref[g] + ob_ref[g // 2, :, half] + oc_ref[:, sl]
    x1 = x_ref[...] + jnp.dot(m_ref[...], wo_ref[...], preferred_element_type=F32)
    x1_ref[...] = x1
    ms = jnp.mean(x1 * x1, axis=-1, keepdims=True)
    h2_ref[...] = (x1 * lax.rsqrt(ms + EPS) * g2_ref[...]).astype(h2_ref.dtype)


def _merge_out(x, out_a, out_b, out_c, w_o, layer, g2):
    T, D = x.shape
    tm = _tile(T, 256)
    row = lambda i: (i, 0)
    return pl.pallas_call(
        _merge_out_kernel,
        grid=(T // tm,),
        in_specs=[pl.BlockSpec((tm, D), row),
                  pl.BlockSpec((out_a.shape[0], tm, HEAD_DIM), lambda i: (0, i, 0)),
                  pl.BlockSpec((out_b.shape[0], tm, 2 * HEAD_DIM), lambda i: (0, i, 0)),
                  pl.BlockSpec((tm, D), row),
                  pl.BlockSpec((None, D, D), lambda i: (layer, 0, 0)),
                  pl.BlockSpec((1, D), lambda i: (0, 0))],
        out_specs=[pl.BlockSpec((tm, D), row), pl.BlockSpec((tm, D), row)],
        out_shape=[jax.ShapeDtypeStruct((T, D), F32), jax.ShapeDtypeStruct((T, D), BF16)],
        scratch_shapes=[pltpu.VMEM((tm, D), BF16)],
        compiler_params=_params("parallel"),
        name="merge_out",
    )(x, out_a, out_b, out_c, w_o, g2.reshape(1, D).astype(F32))


def _mlp_kernel(x1_ref, h2_ref, w1_ref, w2_ref, gn_ref, x2_ref, *hn_ref, with_next):
    f = pl.program_id(1)
    tm = h2_ref.shape[0]
    rc = min(tm, PROJ_ROW_CHUNK)
    chunks = [slice(r * rc, (r + 1) * rc) for r in range(tm // rc)]

    @pl.when(f == 0)
    def _():
        x2_ref[...] = x1_ref[...]

    acts = [jnp.dot(h2_ref[rows, :], w1_ref[...], preferred_element_type=F32) for rows in chunks]
    for rows, a in zip(chunks, acts):
        a = jnp.square(jnp.maximum(a, 0.0)).astype(BF16)
        x2_ref[rows, :] += jnp.dot(a, w2_ref[...], preferred_element_type=F32)

    if with_next:
        @pl.when(f == pl.num_programs(1) - 1)
        def _():
            x2 = x2_ref[...]
            ms = jnp.mean(x2 * x2, axis=-1, keepdims=True)
            hn_ref[0][...] = (x2 * lax.rsqrt(ms + EPS) * gn_ref[...]).astype(BF16)


def _mlp(x1, h2, w1, w2, layer, g_next):
    T, D = x1.shape
    FF = w1.shape[2]
    tm = _tile(T, 512)
    tf = _tile(FF, 1024)
    with_next = g_next is not None
    gn = (g_next if with_next else jnp.ones((D,), F32)).reshape(1, D).astype(F32)
    row = lambda i, f: (i, 0)
    out_specs = [pl.BlockSpec((tm, D), row)]
    out_shape = [jax.ShapeDtypeStruct((T, D), F32)]
    if with_next:
        out_specs.append(pl.BlockSpec((tm, D), row))
        out_shape.append(jax.ShapeDtypeStruct((T, D), BF16))
    out = pl.pallas_call(
        functools.partial(_mlp_kernel, with_next=with_next),
        grid=(T // tm, FF // tf),
        in_specs=[pl.BlockSpec((tm, D), row),
                  pl.BlockSpec((tm, D), row),
                  pl.BlockSpec((None, D, tf), lambda i, f: (layer, 0, f)),
                  pl.BlockSpec((None, tf, D), lambda i, f: (layer, f, 0)),
                  pl.BlockSpec((1, D), lambda i, f: (0, 0))],
        out_specs=out_specs,
        out_shape=out_shape,
        compiler_params=_params("parallel", "arbitrary"),
        name="mlp",
    )(x1, h2, w1, w2, gn)
    return (out[0], out[1]) if with_next else (out[0], None)


def kernel(x, norm1_g, w_in, w_gate, b_gate, a_qn_g, a_kn_g, a_rel_bias, b_qn_g, b_kn_g,
           b_lq1, b_lk1, b_lq2, b_lk2, b_subln_g, c_conv_w, w_o, norm2_g, w_mlp1, w_mlp2):
    B, S, D = x.shape
    depth = w_in.shape[0]
    T = B * S
    b_heads = D // (2 * HEAD_DIM)
    assert S % B_QBLOCK == 0 and D % (2 * HEAD_DIM) == 0 and w_in.shape[2] == 9 * D
    scale = HEAD_DIM ** -0.5 * LOG2E

    xf = x.reshape(T, D).astype(F32)
    w_in, w_gate = w_in.astype(F32), w_gate.astype(F32)
    w_o_b, w1_b, w2_b = w_o.astype(BF16), w_mlp1.astype(BF16), w_mlp2.astype(BF16)

    h = _rmsnorm(xf, norm1_g[0])
    for l in range(depth):
        gains = jnp.stack([a_qn_g[l].astype(F32) * scale, a_kn_g[l].astype(F32),
                           b_qn_g[l].astype(F32) * scale, b_kn_g[l].astype(F32)]).reshape(4, 1, HEAD_DIM)
        qk = _proj_heads(h, w_in, l, (0, 1, 3, 4), seg_cols=D, gains=gains)
        vv = _proj_heads(h, w_in, l, (2, 5), seg_cols=D)
        gates = _proj_heads(h, w_gate, l, (0, 1), seg_cols=D,
                            bias=b_gate[l].reshape(1, -1).astype(F32))
        out_c = _proj_conv(h, w_in, w_gate, l, b_gate[l], c_conv_w[l], seq=S,
                           bg_col=6 * D, cg_col=7 * D, xc_col=8 * D, gc_col=2 * D)

        lambda_init = 0.8 - 0.6 * math.exp(-0.3 * l)
        lam = (jnp.exp(jnp.sum(b_lq1[l].astype(F32) * b_lk1[l].astype(F32)))
               - jnp.exp(jnp.sum(b_lq2[l].astype(F32) * b_lk2[l].astype(F32))) + lambda_init).reshape(1)
        out_a, out_b = _attention(qk, vv, gates, _attn_a_bias_tile(a_rel_bias[l]), lam, b_subln_g[l],
                                  batch=B, seq=S, b_heads=b_heads, out_scale=1.0 - lambda_init)

        x1, h2 = _merge_out(xf, out_a, out_b, out_c, w_o_b, l, norm2_g[l])
        xf, h = _mlp(x1, h2, w1_b, w2_b, l, norm1_g[l + 1] if l + 1 < depth else None)
    return xf.reshape(B, S, D).astype(x.dtype)
```

```python
import functools
import math

import jax
import jax.numpy as jnp
from jax import lax
from jax.experimental import pallas as pl
from jax.experimental.pallas import tpu as pltpu

F32 = jnp.float32
BF16 = jnp.bfloat16

LANE = 128
CHUNK = 64
HEAD_DIM = 128
EPS = 1e-6
A_LEFT_CHUNKS = 8
A_MAX_REL = 128
A_QBLOCK = 2 * CHUNK
A_BAND = (A_LEFT_CHUNKS + 2) * CHUNK
B_QBLOCK = 256
PROJ_ROW_CHUNK = 256
CONV_COL_TILE = 256
MASKED = -1e30
LOG2E = math.log2(math.e)
VMEM_LIMIT = 56 * 1024 * 1024


def _params(*sem):
    return pltpu.CompilerParams(dimension_semantics=sem, vmem_limit_bytes=VMEM_LIMIT)


def _tile(n, pref):
    if n <= pref:
        return n
    t = pref
    while n % t:
        t -= LANE
    assert t > 0
    return t


def _rmsnorm_kernel(x_ref, g_ref, o_ref):
    x = x_ref[...]
    ms = jnp.mean(x * x, axis=-1, keepdims=True)
    o_ref[...] = (x * lax.rsqrt(ms + EPS) * g_ref[...]).astype(o_ref.dtype)


def _rmsnorm(x, g):
    T, D = x.shape
    tm = _tile(T, 512)
    return pl.pallas_call(
        _rmsnorm_kernel,
        grid=(T // tm,),
        in_specs=[pl.BlockSpec((tm, D), lambda i: (i, 0)),
                  pl.BlockSpec((1, D), lambda i: (0, 0))],
        out_specs=pl.BlockSpec((tm, D), lambda i: (i, 0)),
        out_shape=jax.ShapeDtypeStruct((T, D), BF16),
        compiler_params=_params("parallel"),
        name="rmsnorm",
    )(x, g.reshape(1, D).astype(F32))


def _cast_weights(w_refs, wb_ref):
    @pl.when(pl.program_id(1) == 0)
    def _():
        for n, w_ref in enumerate(w_refs):
            wb_ref[n] = w_ref[...].astype(wb_ref.dtype)


def _proj_heads_kernel(h_ref, w_ref, *rest, epilogue):
    p_ref, (o_ref, wb_ref) = rest[0], rest[-2:]
    _cast_weights([w_ref], wb_ref)
    tm = h_ref.shape[0]
    rc = min(tm, PROJ_ROW_CHUNK)
    for r in range(tm // rc):
        rows = slice(r * rc, (r + 1) * rc)
        acc = jnp.dot(h_ref[rows, :], wb_ref[0], preferred_element_type=F32)
        for g in range(o_ref.shape[0]):
            cols = slice(g * LANE, (g + 1) * LANE)
            a = acc[:, cols]
            if epilogue == "norm":
                ms = jnp.mean(a * a, axis=-1, keepdims=True)
                a = a * lax.rsqrt(ms + EPS) * p_ref[...]
            elif epilogue == "gate":
                a = jax.nn.sigmoid(a + p_ref[:, cols])
            o_ref[g, rows, :] = a.astype(o_ref.dtype)


def _proj_heads(h, w, layer, segs, *, seg_cols, gains=None, bias=None):
    T, D = h.shape
    tm = _tile(T, 1024)
    tn = _tile(seg_cols, 1024)
    tps = seg_cols // tn
    n_cols = len(segs) * seg_cols

    def col_block(j):
        k = j // tps
        seg = sum(jnp.where(k == n, s, 0) for n, s in enumerate(segs))
        return seg * tps + j % tps

    if gains is not None:
        epilogue, p, p_spec = "norm", gains, pl.BlockSpec((None, 1, LANE), lambda j, i: (j // tps, 0, 0))
    elif bias is not None:
        epilogue, p, p_spec = "gate", bias, pl.BlockSpec((1, tn), lambda j, i: (0, col_block(j)))
    else:
        epilogue, p, p_spec = "plain", None, None
    return pl.pallas_call(
        functools.partial(_proj_heads_kernel, epilogue=epilogue),
        grid=(n_cols // tn, T // tm),
        in_specs=[pl.BlockSpec((tm, D), lambda j, i: (i, 0)),
                  pl.BlockSpec((None, D, tn), lambda j, i: (layer, 0, col_block(j)))]
                 + ([] if p is None else [p_spec]),
        out_specs=pl.BlockSpec((tn // LANE, tm, LANE), lambda j, i: (j, i, 0)),
        out_shape=jax.ShapeDtypeStruct((n_cols // LANE, T, LANE), BF16),
        scratch_shapes=[pltpu.VMEM((1, D, tn), BF16)],
        compiler_params=_params("arbitrary", "arbitrary"),
        name="proj_heads_" + epilogue,
    )(h, w, *([] if p is None else [p]))


def _proj_conv_kernel(h_ref, wbg_ref, wcg_ref, wxc_ref, wgc_ref, b_ref, cw_ref, o_ref, wb_ref):
    _cast_weights([wbg_ref, wcg_ref, wxc_ref, wgc_ref], wb_ref)
    S = h_ref.shape[0]
    rc = min(S, PROJ_ROW_CHUNK)
    row = lax.broadcasted_iota(jnp.int32, (rc, 1), 0)
    tail = jnp.zeros((8, o_ref.shape[1]), F32)
    for r0 in range(0, S, rc):
        rows = slice(r0, r0 + rc)
        hc = h_ref[rows, :]
        bg, cg, xc, gc = [jnp.dot(hc, wb_ref[n], preferred_element_type=F32) for n in range(4)]
        u = cg * xc
        prev1, prev2 = tail[7:8, :], tail[6:7, :]
        u1 = jnp.where(row == 0, prev1, pltpu.roll(u, 1, 0))
        u2 = jnp.where(row == 0, prev2, jnp.where(row == 1, prev1, pltpu.roll(u, 2, 0)))
        conv = cw_ref[0:1, :] * u2 + cw_ref[1:2, :] * u1 + cw_ref[2:3, :] * u
        o_ref[rows, :] = (jax.nn.sigmoid(gc + b_ref[...]) * (bg * conv)).astype(o_ref.dtype)
        tail = u[rc - 8:, :]


def _proj_conv(h, w_in, w_gate, layer, b_gate, conv_w, *, seq, bg_col, cg_col, xc_col, gc_col):
    T, D = h.shape
    tn = _tile(D, CONV_COL_TILE)

    def w_spec(col0):
        return pl.BlockSpec((None, D, tn), lambda j, i: (layer, 0, col0 // tn + j))

    return pl.pallas_call(
        _proj_conv_kernel,
        grid=(D // tn, T // seq),
        in_specs=[pl.BlockSpec((seq, D), lambda j, i: (i, 0)),
                  w_spec(bg_col), w_spec(cg_col), w_spec(xc_col), w_spec(gc_col),
                  pl.BlockSpec((1, tn), lambda j, i: (0, gc_col // tn + j)),
                  pl.BlockSpec((3, tn), lambda j, i: (0, j))],
        out_specs=pl.BlockSpec((seq, tn), lambda j, i: (i, j)),
        out_shape=jax.ShapeDtypeStruct((T, D), BF16),
        scratch_shapes=[pltpu.VMEM((4, D, tn), BF16)],
        compiler_params=_params("arbitrary", "arbitrary"),
        name="proj_conv",
    )(h, w_in, w_in, w_in, w_gate, b_gate.reshape(1, -1).astype(F32), conv_w.astype(F32))


def _attn_a_bias_tile(rel_bias):
    H = rel_bias.shape[0]
    table = rel_bias.astype(F32)
    n_far = A_LEFT_CHUNKS * CHUNK - A_MAX_REL + CHUNK - 1
    far = jnp.broadcast_to(table[:, -1:], (H, n_far))
    near = table[:, :A_MAX_REL - CHUNK:-1]
    f = jnp.concatenate([far, near, jnp.zeros((H, 1), F32)], axis=1)
    L = f.shape[1] - 1
    windows = jnp.tile(f, (1, CHUNK))[:, :CHUNK * L].reshape(H, CHUNK, L)
    t = windows[:, :, CHUNK - 1:]
    masked = jnp.full((H, CHUNK, CHUNK), MASKED, F32)
    t = t * LOG2E
    return jnp.concatenate([jnp.concatenate([t, masked], axis=2),
                            jnp.concatenate([masked, t], axis=2)], axis=1)


def _attn_a_steps(q_ref, k_ref, v_ref, bias_ref, gate_ref, o_ref, vx_ref):
    S = q_ref.shape[0]
    n_blocks = S // A_QBLOCK
    vx_ref[:, :HEAD_DIM] = v_ref[...]
    vx_ref[:, HEAD_DIM:] = jnp.ones((S, HEAD_DIM), vx_ref.dtype)

    def band(blk):
        k1 = (blk + 1) * A_QBLOCK
        return max(0, k1 - A_BAND), k1

    def scores(blk):
        k0, k1 = band(blk)
        b0 = A_BAND - (k1 - k0)
        s = lax.dot_general(q_ref[blk * A_QBLOCK:k1, :], k_ref[k0:k1, :], (((1,), (1,)), ((), ())),
                            preferred_element_type=F32)
        return s + bias_ref[:, b0:]

    s_next = scores(0)
    for blk in range(n_blocks):
        s = s_next
        if blk + 1 < n_blocks:
            s_next = scores(blk + 1)
        k0, k1 = band(blk)
        e = jnp.exp2(s - jnp.max(s, axis=-1, keepdims=True))
        ox = jnp.dot(e.astype(BF16), vx_ref[k0:k1, :], preferred_element_type=F32)
        rows = slice(blk * A_QBLOCK, k1)
        o_ref[rows, :] = (ox[:, :HEAD_DIM] / ox[:, HEAD_DIM:]).astype(o_ref.dtype) * gate_ref[rows, :]
        yield


def _attn_b_steps(lam_ref, q_ref, k_ref, v_ref, g_ref, gate_ref, o_ref, *, out_scale):
    S = q_ref.shape[1]
    lam = lam_ref[0]
    v = jnp.concatenate([v_ref[0], v_ref[1]], axis=-1)
    row_chunk = lax.broadcasted_iota(jnp.int32, (B_QBLOCK, B_QBLOCK), 0) // CHUNK
    col_chunk = lax.broadcasted_iota(jnp.int32, (B_QBLOCK, B_QBLOCK), 1) // CHUNK
    diag_mask = jnp.where(col_chunk <= row_chunk, 0.0, MASKED).astype(F32)
    nt = (((1,), (1,)), ((), ()))

    def scores(blk):
        q0, q1 = blk * B_QBLOCK, (blk + 1) * B_QBLOCK
        out = []
        for m in range(2):
            q = q_ref[m, q0:q1, :]
            sd = lax.dot_general(q, k_ref[m, q0:q1, :], nt, preferred_element_type=F32) + diag_mask
            sl = lax.dot_general(q, k_ref[m, 0:q0, :], nt, preferred_element_type=F32) if blk else None
            out.append((sl, sd))
        return out

    def softmax_parts(sl, sd):
        mx = jnp.max(sd, axis=-1, keepdims=True)
        el = None
        if sl is not None:
            mx = jnp.maximum(mx, jnp.max(sl, axis=-1, keepdims=True))
            el = jnp.exp2(sl - mx)
        ed = jnp.exp2(sd - mx)
        l = jnp.sum(ed, axis=-1, keepdims=True)
        if sl is not None:
            l = l + jnp.sum(el, axis=-1, keepdims=True)
        return el, ed, l

    n_blocks = S // B_QBLOCK
    s_next = scores(0)
    for blk in range(n_blocks):
        q0, q1 = blk * B_QBLOCK, (blk + 1) * B_QBLOCK
        (sl1, sd1), (sl2, sd2) = s_next
        if blk + 1 < n_blocks:
            s_next = scores(blk + 1)
        yield
        el1, ed1, l1 = softmax_parts(sl1, sd1)
        el2, ed2, l2 = softmax_parts(sl2, sd2)
        r = lam * l1 / l2
        ob = jnp.dot((ed1 - ed2 * r).astype(BF16), v[q0:q1, :], preferred_element_type=F32)
        if blk:
            ob = ob + jnp.dot((el1 - el2 * r).astype(BF16), v[0:q0, :], preferred_element_type=F32)
        ob = ob / l1
        ms = jnp.mean(ob * ob, axis=-1, keepdims=True)
        y = (ob * lax.rsqrt(ms + EPS) * g_ref[...] * out_scale).astype(o_ref.dtype)
        for half in range(2):
            cols = slice(half * HEAD_DIM, (half + 1) * HEAD_DIM)
            o_ref[q0:q1, cols] = y[:, cols] * gate_ref[half, q0:q1, :]
        yield


def _attn_kernel(lam_ref, qa_ref, ka_ref, va_ref, bias_ref, ga_ref, qb_ref, kb_ref, vb_ref, g_ref, gb_ref,
                 oa_ref, ob_ref, vx_ref, *, out_scale):
    a_progs = [_attn_a_steps(qa_ref.at[i], ka_ref.at[i], va_ref.at[i], bias_ref.at[i], ga_ref.at[i],
                             oa_ref.at[i], vx_ref.at[i]) for i in range(qa_ref.shape[0])]
    b_prog = _attn_b_steps(lam_ref, qb_ref, kb_ref, vb_ref, g_ref, gb_ref, ob_ref, out_scale=out_scale)
    S = qa_ref.shape[1]
    a_per_b = (S // A_QBLOCK) // (S // B_QBLOCK)
    for _ in range(S // B_QBLOCK):
        next(b_prog)
        for _ in range(a_per_b):
            for prog in a_progs:
                next(prog)
        next(b_prog)


def _attention(qk, vv, gates, bias_tile, lam, subln_g, *, batch, seq, b_heads, out_scale):
    T = qk.shape[1]

    def pair(seg):
        return pl.BlockSpec((2, seq, HEAD_DIM), lambda b, h: (seg * b_heads + h, b, 0))

    return pl.pallas_call(
        functools.partial(_attn_kernel, out_scale=out_scale),
        grid=(batch, b_heads),
        in_specs=[pl.BlockSpec(memory_space=pltpu.SMEM),
                  pair(0), pair(1), pair(0),
                  pl.BlockSpec((2, A_QBLOCK, A_BAND), lambda b, h: (h, 0, 0)), pair(0),
                  pair(2), pair(3), pair(1),
                  pl.BlockSpec((1, 2 * HEAD_DIM), lambda b, h: (0, 0)), pair(1)],
        out_specs=[pl.BlockSpec((2, seq, HEAD_DIM), lambda b, h: (h, b, 0)),
                   pl.BlockSpec((None, seq, 2 * HEAD_DIM), lambda b, h: (h, b, 0))],
        out_shape=[jax.ShapeDtypeStruct((2 * b_heads, T, HEAD_DIM), BF16),
                   jax.ShapeDtypeStruct((b_heads, T, 2 * HEAD_DIM), BF16)],
        scratch_shapes=[pltpu.VMEM((2, seq, 2 * HEAD_DIM), BF16)],
        compiler_params=_params("parallel", "parallel"),
        name="attention",
    )(lam, qk, qk, vv, bias_tile, gates, qk, qk, vv, subln_g.reshape(1, 2 * HEAD_DIM).astype(F32), gates)


def _merge_out_kernel(x_ref, oa_ref, ob_ref, oc_ref, wo_ref, g2_ref, x1_ref, h2_ref, m_ref):
    D = x_ref.shape[1]
    for g in range(D // LANE):
        sl = slice(g * LANE, (g + 1) * LANE)
        half = slice((g % 2) * LANE, (g % 2 + 1) * LANE)
        m_ref[:, sl] = oa_ref[g] + ob_ref[g // 2, :, half] + oc_ref[:, sl]
    x1 = x_ref[...] + jnp.dot(m_ref[...], wo_ref[...], preferred_element_type=F32)
    x1_ref[...] = x1
    ms = jnp.mean(x1 * x1, axis=-1, keepdims=True)
    h2_ref[...] = (x1 * lax.rsqrt(ms + EPS) * g2_ref[...]).astype(h2_ref.dtype)


def _merge_out(x, out_a, out_b, out_c, w_o, layer, g2):
    T, D = x.shape
    tm = _tile(T, 512)
    row = lambda i: (i, 0)
    once = pl.Buffered(1)
    return pl.pallas_call(
        _merge_out_kernel,
        grid=(T // tm,),
        in_specs=[pl.BlockSpec((tm, D), row),
                  pl.BlockSpec((out_a.shape[0], tm, HEAD_DIM), lambda i: (0, i, 0)),
                  pl.BlockSpec((out_b.shape[0], tm, 2 * HEAD_DIM), lambda i: (0, i, 0)),
                  pl.BlockSpec((tm, D), row),
                  pl.BlockSpec((None, D, D), lambda i: (layer, 0, 0), pipeline_mode=once),
                  pl.BlockSpec((1, D), lambda i: (0, 0), pipeline_mode=once)],
        out_specs=[pl.BlockSpec((tm, D), row), pl.BlockSpec((tm, D), row)],
        out_shape=[jax.ShapeDtypeStruct((T, D), F32), jax.ShapeDtypeStruct((T, D), BF16)],
        scratch_shapes=[pltpu.VMEM((tm, D), BF16)],
        compiler_params=_params("parallel"),
        name="merge_out",
    )(x, out_a, out_b, out_c, w_o, g2.reshape(1, D).astype(F32))


def _mlp_kernel(x1_ref, h2_ref, w1_ref, w2_ref, gn_ref, x2_ref, *hn_ref, with_next):
    f = pl.program_id(1)
    tm = h2_ref.shape[0]
    rc = min(tm, PROJ_ROW_CHUNK)
    chunks = [slice(r * rc, (r + 1) * rc) for r in range(tm // rc)]

    @pl.when(f == 0)
    def _():
        x2_ref[...] = x1_ref[...]

    acts = [jnp.dot(h2_ref[rows, :], w1_ref[...], preferred_element_type=F32) for rows in chunks]
    for rows, a in zip(chunks, acts):
        a = jnp.square(jnp.maximum(a, 0.0)).astype(BF16)
        x2_ref[rows, :] += jnp.dot(a, w2_ref[...], preferred_element_type=F32)

    if with_next:
        @pl.when(f == pl.num_programs(1) - 1)
        def _():
            x2 = x2_ref[...]
            ms = jnp.mean(x2 * x2, axis=-1, keepdims=True)
            hn_ref[0][...] = (x2 * lax.rsqrt(ms + EPS) * gn_ref[...]).astype(BF16)


def _mlp(x1, h2, w1, w2, layer, g_next):
    T, D = x1.shape
    FF = w1.shape[2]
    tm = _tile(T, 512)
    tf = _tile(FF, 1024)
    with_next = g_next is not None
    gn = (g_next if with_next else jnp.ones((D,), F32)).reshape(1, D).astype(F32)
    row = lambda i, f: (i, 0)
    out_specs = [pl.BlockSpec((tm, D), row)]
    out_shape = [jax.ShapeDtypeStruct((T, D), F32)]
    if with_next:
        out_specs.append(pl.BlockSpec((tm, D), row))
        out_shape.append(jax.ShapeDtypeStruct((T, D), BF16))
    out = pl.pallas_call(
        functools.partial(_mlp_kernel, with_next=with_next),
        grid=(T // tm, FF // tf),
        in_specs=[pl.BlockSpec((tm, D), row),
                  pl.BlockSpec((tm, D), row),
                  pl.BlockSpec((None, D, tf), lambda i, f: (layer, 0, f)),
                  pl.BlockSpec((None, tf, D), lambda i, f: (layer, f, 0)),
                  pl.BlockSpec((1, D), lambda i, f: (0, 0))],
        out_specs=out_specs,
        out_shape=out_shape,
        compiler_params=_params("parallel", "arbitrary"),
        name="mlp",
    )(x1, h2, w1, w2, gn)
    return (out[0], out[1]) if with_next else (out[0], None)


def kernel(x, norm1_g, w_in, w_gate, b_gate, a_qn_g, a_kn_g, a_rel_bias, b_qn_g, b_kn_g,
           b_lq1, b_lk1, b_lq2, b_lk2, b_subln_g, c_conv_w, w_o, norm2_g, w_mlp1, w_mlp2):
    B, S, D = x.shape
    depth = w_in.shape[0]
    T = B * S
    b_heads = D // (2 * HEAD_DIM)
    assert S % B_QBLOCK == 0 and D % (2 * HEAD_DIM) == 0 and w_in.shape[2] == 9 * D
    scale = HEAD_DIM ** -0.5 * LOG2E

    xf = x.reshape(T, D).astype(F32)
    w_in, w_gate = w_in.astype(F32), w_gate.astype(F32)
    w_o_b, w1_b, w2_b = w_o.astype(BF16), w_mlp1.astype(BF16), w_mlp2.astype(BF16)

    h = _rmsnorm(xf, norm1_g[0])
    for l in range(depth):
        gains = jnp.stack([a_qn_g[l].astype(F32) * scale, a_kn_g[l].astype(F32),
                           b_qn_g[l].astype(F32) * scale, b_kn_g[l].astype(F32)]).reshape(4, 1, HEAD_DIM)
        qk = _proj_heads(h, w_in, l, (0, 1, 3, 4), seg_cols=D, gains=gains)
        vv = _proj_heads(h, w_in, l, (2, 5), seg_cols=D)
        gates = _proj_heads(h, w_gate, l, (0, 1), seg_cols=D,
                            bias=b_gate[l].reshape(1, -1).astype(F32))
        out_c = _proj_conv(h, w_in, w_gate, l, b_gate[l], c_conv_w[l], seq=S,
                           bg_col=6 * D, cg_col=7 * D, xc_col=8 * D, gc_col=2 * D)

        lambda_init = 0.8 - 0.6 * math.exp(-0.3 * l)
        lam = (jnp.exp(jnp.sum(b_lq1[l].astype(F32) * b_lk1[l].astype(F32)))
               - jnp.exp(jnp.sum(b_lq2[l].astype(F32) * b_lk2[l].astype(F32))) + lambda_init).reshape(1)
        out_a, out_b = _attention(qk, vv, gates, _attn_a_bias_tile(a_rel_bias[l]), lam, b_subln_g[l],
                                  batch=B, seq=S, b_heads=b_heads, out_scale=1.0 - lambda_init)

        x1, h2 = _merge_out(xf, out_a, out_b, out_c, w_o_b, l, norm2_g[l])
        xf, h = _mlp(x1, h2, w1_b, w2_b, l, norm1_g[l + 1] if l + 1 < depth else None)
    return xf.reshape(B, S, D).astype(x.dtype)
```

```python
import functools
import math

import jax
import jax.numpy as jnp
from jax import lax
from jax.experimental import pallas as pl
from jax.experimental.pallas import tpu as pltpu

F32 = jnp.float32
BF16 = jnp.bfloat16

LANE = 128
CHUNK = 64
HEAD_DIM = 128
EPS = 1e-6
A_LEFT_CHUNKS = 8
A_MAX_REL = 128
A_QBLOCK = 2 * CHUNK
A_BAND = (A_LEFT_CHUNKS + 2) * CHUNK
B_QBLOCK = 256
PROJ_ROW_CHUNK = 256
CONV_COL_TILE = 256
MASKED = -1e30
LOG2E = math.log2(math.e)
VMEM_LIMIT = 56 * 1024 * 1024


def _params(*sem):
    return pltpu.CompilerParams(dimension_semantics=sem, vmem_limit_bytes=VMEM_LIMIT)


def _tile(n, pref):
    if n <= pref:
        return n
    t = pref
    while n % t:
        t -= LANE
    assert t > 0
    return t


def _rmsnorm_kernel(x_ref, g_ref, o_ref):
    x = x_ref[...]
    ms = jnp.mean(x * x, axis=-1, keepdims=True)
    o_ref[...] = (x * lax.rsqrt(ms + EPS) * g_ref[...]).astype(o_ref.dtype)


def _rmsnorm(x, g):
    T, D = x.shape
    tm = _tile(T, 512)
    return pl.pallas_call(
        _rmsnorm_kernel,
        grid=(T // tm,),
        in_specs=[pl.BlockSpec((tm, D), lambda i: (i, 0)),
                  pl.BlockSpec((1, D), lambda i: (0, 0))],
        out_specs=pl.BlockSpec((tm, D), lambda i: (i, 0)),
        out_shape=jax.ShapeDtypeStruct((T, D), BF16),
        compiler_params=_params("parallel"),
        name="rmsnorm",
    )(x, g.reshape(1, D).astype(F32))


def _cast_weights(w_refs, wb_ref):
    @pl.when(pl.program_id(1) == 0)
    def _():
        for n, w_ref in enumerate(w_refs):
            wb_ref[n] = w_ref[...].astype(wb_ref.dtype)


def _proj_heads_kernel(h_ref, w_ref, *rest, epilogue):
    p_ref, (o_ref, wb_ref) = rest[0], rest[-2:]
    _cast_weights([w_ref], wb_ref)
    tm = h_ref.shape[0]
    rc = min(tm, PROJ_ROW_CHUNK)
    for r in range(tm // rc):
        rows = slice(r * rc, (r + 1) * rc)
        acc = jnp.dot(h_ref[rows, :], wb_ref[0], preferred_element_type=F32)
        for g in range(o_ref.shape[0]):
            cols = slice(g * LANE, (g + 1) * LANE)
            a = acc[:, cols]
            if epilogue == "norm":
                ms = jnp.mean(a * a, axis=-1, keepdims=True)
                a = a * lax.rsqrt(ms + EPS) * p_ref[...]
            elif epilogue == "gate":
                a = jax.nn.sigmoid(a + p_ref[:, cols])
            o_ref[g, rows, :] = a.astype(o_ref.dtype)


def _proj_heads(h, w, layer, segs, *, seg_cols, gains=None, bias=None):
    T, D = h.shape
    tm = _tile(T, 2048)
    tn = _tile(seg_cols, 1024)
    tps = seg_cols // tn
    n_cols = len(segs) * seg_cols

    def col_block(j):
        k = j // tps
        seg = sum(jnp.where(k == n, s, 0) for n, s in enumerate(segs))
        return seg * tps + j % tps

    if gains is not None:
        epilogue, p, p_spec = "norm", gains, pl.BlockSpec((None, 1, LANE), lambda j, i: (j // tps, 0, 0))
    elif bias is not None:
        epilogue, p, p_spec = "gate", bias, pl.BlockSpec((1, tn), lambda j, i: (0, col_block(j)))
    else:
        epilogue, p, p_spec = "plain", None, None
    return pl.pallas_call(
        functools.partial(_proj_heads_kernel, epilogue=epilogue),
        grid=(n_cols // tn, T // tm),
        in_specs=[pl.BlockSpec((tm, D), lambda j, i: (i, 0)),
                  pl.BlockSpec((None, D, tn), lambda j, i: (layer, 0, col_block(j)))]
                 + ([] if p is None else [p_spec]),
        out_specs=pl.BlockSpec((tn // LANE, tm, LANE), lambda j, i: (j, i, 0)),
        out_shape=jax.ShapeDtypeStruct((n_cols // LANE, T, LANE), BF16),
        scratch_shapes=[pltpu.VMEM((1, D, tn), BF16)],
        compiler_params=_params("arbitrary", "arbitrary"),
        name="proj_heads_" + epilogue,
    )(h, w, *([] if p is None else [p]))


def _proj_conv_kernel(h_ref, wbg_ref, wcg_ref, wxc_ref, wgc_ref, b_ref, cw_ref, o_ref, wb_ref):
    _cast_weights([wbg_ref, wcg_ref, wxc_ref, wgc_ref], wb_ref)
    S = h_ref.shape[0]
    rc = min(S, PROJ_ROW_CHUNK)
    row = lax.broadcasted_iota(jnp.int32, (rc, 1), 0)
    tail = jnp.zeros((8, o_ref.shape[1]), F32)
    for r0 in range(0, S, rc):
        rows = slice(r0, r0 + rc)
        hc = h_ref[rows, :]
        bg, cg, xc, gc = [jnp.dot(hc, wb_ref[n], preferred_element_type=F32) for n in range(4)]
        u = cg * xc
        prev1, prev2 = tail[7:8, :], tail[6:7, :]
        u1 = jnp.where(row == 0, prev1, pltpu.roll(u, 1, 0))
        u2 = jnp.where(row == 0, prev2, jnp.where(row == 1, prev1, pltpu.roll(u, 2, 0)))
        conv = cw_ref[0:1, :] * u2 + cw_ref[1:2, :] * u1 + cw_ref[2:3, :] * u
        o_ref[rows, :] = (jax.nn.sigmoid(gc + b_ref[...]) * (bg * conv)).astype(o_ref.dtype)
        tail = u[rc - 8:, :]


def _proj_conv(h, w_in, w_gate, layer, b_gate, conv_w, *, seq, bg_col, cg_col, xc_col, gc_col):
    T, D = h.shape
    tn = _tile(D, CONV_COL_TILE)

    def w_spec(col0):
        return pl.BlockSpec((None, D, tn), lambda j, i: (layer, 0, col0 // tn + j))

    return pl.pallas_call(
        _proj_conv_kernel,
        grid=(D // tn, T // seq),
        in_specs=[pl.BlockSpec((seq, D), lambda j, i: (i, 0)),
                  w_spec(bg_col), w_spec(cg_col), w_spec(xc_col), w_spec(gc_col),
                  pl.BlockSpec((1, tn), lambda j, i: (0, gc_col // tn + j)),
                  pl.BlockSpec((3, tn), lambda j, i: (0, j))],
        out_specs=pl.BlockSpec((seq, tn), lambda j, i: (i, j)),
        out_shape=jax.ShapeDtypeStruct((T, D), BF16),
        scratch_shapes=[pltpu.VMEM((4, D, tn), BF16)],
        compiler_params=_params("arbitrary", "arbitrary"),
        name="proj_conv",
    )(h, w_in, w_in, w_in, w_gate, b_gate.reshape(1, -1).astype(F32), conv_w.astype(F32))


def _attn_a_bias_tile(rel_bias):
    H = rel_bias.shape[0]
    table = rel_bias.astype(F32)
    n_far = A_LEFT_CHUNKS * CHUNK - A_MAX_REL + CHUNK - 1
    far = jnp.broadcast_to(table[:, -1:], (H, n_far))
    near = table[:, :A_MAX_REL - CHUNK:-1]
    f = jnp.concatenate([far, near, jnp.zeros((H, 1), F32)], axis=1)
    L = f.shape[1] - 1
    windows = jnp.tile(f, (1, CHUNK))[:, :CHUNK * L].reshape(H, CHUNK, L)
    t = windows[:, :, CHUNK - 1:]
    masked = jnp.full((H, CHUNK, CHUNK), MASKED, F32)
    t = t * LOG2E
    return jnp.concatenate([jnp.concatenate([t, masked], axis=2),
                            jnp.concatenate([masked, t], axis=2)], axis=1)


def _attn_a_steps(q_ref, k_ref, v_ref, bias_ref, gate_ref, o_ref, vx_ref):
    S = q_ref.shape[0]
    n_blocks = S // A_QBLOCK
    vx_ref[:, :HEAD_DIM] = v_ref[...]
    vx_ref[:, HEAD_DIM:] = jnp.ones((S, HEAD_DIM), vx_ref.dtype)

    def band(blk):
        k1 = (blk + 1) * A_QBLOCK
        return max(0, k1 - A_BAND), k1

    def scores(blk):
        k0, k1 = band(blk)
        b0 = A_BAND - (k1 - k0)
        s = lax.dot_general(q_ref[blk * A_QBLOCK:k1, :], k_ref[k0:k1, :], (((1,), (1,)), ((), ())),
                            preferred_element_type=F32)
        return s + bias_ref[:, b0:]

    s_next = scores(0)
    for blk in range(n_blocks):
        s = s_next
        if blk + 1 < n_blocks:
            s_next = scores(blk + 1)
        k0, k1 = band(blk)
        e = jnp.exp2(s - jnp.max(s, axis=-1, keepdims=True))
        ox = jnp.dot(e.astype(BF16), vx_ref[k0:k1, :], preferred_element_type=F32)
        rows = slice(blk * A_QBLOCK, k1)
        o_ref[rows, :] = (ox[:, :HEAD_DIM] / ox[:, HEAD_DIM:]).astype(o_ref.dtype) * gate_ref[rows, :]
        yield


def _attn_b_steps(lam_ref, q_ref, k_ref, v_ref, g_ref, gate_ref, o_ref, *, out_scale):
    S = q_ref.shape[1]
    lam = lam_ref[0]
    v = jnp.concatenate([v_ref[0], v_ref[1]], axis=-1)
    chunk = lax.broadcasted_iota(jnp.int32, (B_QBLOCK, HEAD_DIM), 0) // CHUNK
    lane = lax.broadcasted_iota(jnp.int32, (B_QBLOCK, HEAD_DIM), 1)
    q_hot = jnp.where(lane == chunk, 1.0, 0.0).astype(q_ref.dtype)
    k_hot = jnp.where((lane < B_QBLOCK // CHUNK) & (chunk > lane), MASKED, 0.0).astype(k_ref.dtype)
    gs = g_ref[...] * out_scale
    nt = (((1,), (1,)), ((), ()))

    def scores(blk):
        q0, q1 = blk * B_QBLOCK, (blk + 1) * B_QBLOCK
        out = []
        for m in range(2):
            q = q_ref[m, q0:q1, :]
            sd = lax.dot_general(jnp.concatenate([q, q_hot], axis=1),
                                 jnp.concatenate([k_ref[m, q0:q1, :], k_hot], axis=1), nt,
                                 preferred_element_type=F32)
            sl = lax.dot_general(q, k_ref[m, 0:q0, :], nt, preferred_element_type=F32) if blk else None
            out.append((sl, sd))
        return out

    def softmax_parts(sl, sd):
        mx = jnp.max(sd, axis=-1, keepdims=True)
        el = None
        if sl is not None:
            mx = jnp.maximum(mx, jnp.max(sl, axis=-1, keepdims=True))
            el = jnp.exp2(sl - mx)
        ed = jnp.exp2(sd - mx)
        l = jnp.sum(ed, axis=-1, keepdims=True)
        if sl is not None:
            l = l + jnp.sum(el, axis=-1, keepdims=True)
        return el, ed, l

    n_blocks = S // B_QBLOCK
    s_next = scores(0)
    for blk in range(n_blocks):
        q0, q1 = blk * B_QBLOCK, (blk + 1) * B_QBLOCK
        (sl1, sd1), (sl2, sd2) = s_next
        if blk + 1 < n_blocks:
            s_next = scores(blk + 1)
        yield
        el1, ed1, l1 = softmax_parts(sl1, sd1)
        el2, ed2, l2 = softmax_parts(sl2, sd2)
        r = lam * l1 / l2
        ob = jnp.dot((ed1 - ed2 * r).astype(BF16), v[q0:q1, :], preferred_element_type=F32)
        if blk:
            ob = ob + jnp.dot((el1 - el2 * r).astype(BF16), v[0:q0, :], preferred_element_type=F32)
        inv_l1 = 1.0 / l1
        ms = jnp.mean(ob * ob, axis=-1, keepdims=True) * (inv_l1 * inv_l1)
        y = (ob * (inv_l1 * lax.rsqrt(ms + EPS)) * gs).astype(o_ref.dtype)
        for half in range(2):
            cols = slice(half * HEAD_DIM, (half + 1) * HEAD_DIM)
            o_ref[q0:q1, cols] = y[:, cols] * gate_ref[half, q0:q1, :]
        yield


def _attn_kernel(lam_ref, qa_ref, ka_ref, va_ref, bias_ref, ga_ref, qb_ref, kb_ref, vb_ref, g_ref, gb_ref,
                 oa_ref, ob_ref, vx_ref, *, out_scale):
    a_progs = [_attn_a_steps(qa_ref.at[i], ka_ref.at[i], va_ref.at[i], bias_ref.at[i], ga_ref.at[i],
                             oa_ref.at[i], vx_ref.at[i]) for i in range(qa_ref.shape[0])]
    b_prog = _attn_b_steps(lam_ref, qb_ref, kb_ref, vb_ref, g_ref, gb_ref, ob_ref, out_scale=out_scale)
    S = qa_ref.shape[1]
    a_per_b = (S // A_QBLOCK) // (S // B_QBLOCK)
    for _ in range(S // B_QBLOCK):
        next(b_prog)
        for _ in range(a_per_b):
            for prog in a_progs:
                next(prog)
        next(b_prog)


def _attention(qk, vv, gates, bias_tile, lam, subln_g, *, batch, seq, b_heads, out_scale):
    T = qk.shape[1]

    def pair(seg):
        return pl.BlockSpec((2, seq, HEAD_DIM), lambda b, h: (seg * b_heads + h, b, 0))

    return pl.pallas_call(
        functools.partial(_attn_kernel, out_scale=out_scale),
        grid=(batch, b_heads),
        in_specs=[pl.BlockSpec(memory_space=pltpu.SMEM),
                  pair(0), pair(1), pair(0),
                  pl.BlockSpec((2, A_QBLOCK, A_BAND), lambda b, h: (h, 0, 0)), pair(0),
                  pair(2), pair(3), pair(1),
                  pl.BlockSpec((1, 2 * HEAD_DIM), lambda b, h: (0, 0)), pair(1)],
        out_specs=[pl.BlockSpec((2, seq, HEAD_DIM), lambda b, h: (h, b, 0)),
                   pl.BlockSpec((None, seq, 2 * HEAD_DIM), lambda b, h: (h, b, 0))],
        out_shape=[jax.ShapeDtypeStruct((2 * b_heads, T, HEAD_DIM), BF16),
                   jax.ShapeDtypeStruct((b_heads, T, 2 * HEAD_DIM), BF16)],
        scratch_shapes=[pltpu.VMEM((2, seq, 2 * HEAD_DIM), BF16)],
        compiler_params=_params("parallel", "parallel"),
        name="attention",
    )(lam, qk, qk, vv, bias_tile, gates, qk, qk, vv, subln_g.reshape(1, 2 * HEAD_DIM).astype(F32), gates)


def _merge_out_kernel(x_ref, oa_ref, ob_ref, oc_ref, wo_ref, g2_ref, x1_ref, h2_ref, m_ref):
    D = x_ref.shape[1]
    for g in range(D // LANE):
        sl = slice(g * LANE, (g + 1) * LANE)
        half = slice((g % 2) * LANE, (g % 2 + 1) * LANE)
        m_ref[:, sl] = oa_ref[g] + ob_ref[g // 2, :, half] + oc_ref[:, sl]
    x1 = x_ref[...] + jnp.dot(m_ref[...], wo_ref[...], preferred_element_type=F32)
    x1_ref[...] = x1
    ms = jnp.mean(x1 * x1, axis=-1, keepdims=True)
    h2_ref[...] = (x1 * lax.rsqrt(ms + EPS) * g2_ref[...]).astype(h2_ref.dtype)


def _merge_out(x, out_a, out_b, out_c, w_o, layer, g2):
    T, D = x.shape
    tm = _tile(T, 512)
    row = lambda i: (i, 0)
    once = pl.Buffered(1)
    return pl.pallas_call(
        _merge_out_kernel,
        grid=(T // tm,),
        in_specs=[pl.BlockSpec((tm, D), row),
                  pl.BlockSpec((out_a.shape[0], tm, HEAD_DIM), lambda i: (0, i, 0)),
                  pl.BlockSpec((out_b.shape[0], tm, 2 * HEAD_DIM), lambda i: (0, i, 0)),
                  pl.BlockSpec((tm, D), row),
                  pl.BlockSpec((None, D, D), lambda i: (layer, 0, 0), pipeline_mode=once),
                  pl.BlockSpec((1, D), lambda i: (0, 0), pipeline_mode=once)],
        out_specs=[pl.BlockSpec((tm, D), row), pl.BlockSpec((tm, D), row)],
        out_shape=[jax.ShapeDtypeStruct((T, D), F32), jax.ShapeDtypeStruct((T, D), BF16)],
        scratch_shapes=[pltpu.VMEM((tm, D), BF16)],
        compiler_params=_params("parallel"),
        name="merge_out",
    )(x, out_a, out_b, out_c, w_o, g2.reshape(1, D).astype(F32))


def _mlp_kernel(x1_ref, h2_ref, w1_ref, w2_ref, gn_ref, x2_ref, *hn_ref, with_next):
    f = pl.program_id(1)
    tm = h2_ref.shape[0]
    rc = min(tm, PROJ_ROW_CHUNK)
    chunks = [slice(r * rc, (r + 1) * rc) for r in range(tm // rc)]

    @pl.when(f == 0)
    def _():
        x2_ref[...] = x1_ref[...]

    acts = [jnp.dot(h2_ref[rows, :], w1_ref[...], preferred_element_type=F32) for rows in chunks]
    for rows, a in zip(chunks, acts):
        a = jnp.square(jnp.maximum(a, 0.0)).astype(BF16)
        x2_ref[rows, :] += jnp.dot(a, w2_ref[...], preferred_element_type=F32)

    if with_next:
        @pl.when(f == pl.num_programs(1) - 1)
        def _():
            x2 = x2_ref[...]
            ms = jnp.mean(x2 * x2, axis=-1, keepdims=True)
            hn_ref[0][...] = (x2 * lax.rsqrt(ms + EPS) * gn_ref[...]).astype(BF16)


def _mlp(x1, h2, w1, w2, layer, g_next):
    T, D = x1.shape
    FF = w1.shape[2]
    tm = _tile(T, 512)
    tf = _tile(FF, 1024)
    with_next = g_next is not None
    gn = (g_next if with_next else jnp.ones((D,), F32)).reshape(1, D).astype(F32)
    row = lambda i, f: (i, 0)
    out_specs = [pl.BlockSpec((tm, D), row)]
    out_shape = [jax.ShapeDtypeStruct((T, D), F32)]
    if with_next:
        out_specs.append(pl.BlockSpec((tm, D), row))
        out_shape.append(jax.ShapeDtypeStruct((T, D), BF16))
    out = pl.pallas_call(
        functools.partial(_mlp_kernel, with_next=with_next),
        grid=(T // tm, FF // tf),
        in_specs=[pl.BlockSpec((tm, D), row),
                  pl.BlockSpec((tm, D), row),
                  pl.BlockSpec((None, D, tf), lambda i, f: (layer, 0, f)),
                  pl.BlockSpec((None, tf, D), lambda i, f: (layer, f, 0)),
                  pl.BlockSpec((1, D), lambda i, f: (0, 0))],
        out_specs=out_specs,
        out_shape=out_shape,
        compiler_params=_params("parallel", "arbitrary"),
        name="mlp",
    )(x1, h2, w1, w2, gn)
    return (out[0], out[1]) if with_next else (out[0], None)


def kernel(x, norm1_g, w_in, w_gate, b_gate, a_qn_g, a_kn_g, a_rel_bias, b_qn_g, b_kn_g,
           b_lq1, b_lk1, b_lq2, b_lk2, b_subln_g, c_conv_w, w_o, norm2_g, w_mlp1, w_mlp2):
    B, S, D = x.shape
    depth = w_in.shape[0]
    T = B * S
    b_heads = D // (2 * HEAD_DIM)
    assert S % B_QBLOCK == 0 and D % (2 * HEAD_DIM) == 0 and w_in.shape[2] == 9 * D
    scale = HEAD_DIM ** -0.5 * LOG2E

    xf = x.reshape(T, D).astype(F32)
    w_in, w_gate = w_in.astype(F32), w_gate.astype(F32)
    w_o_b, w1_b, w2_b = w_o.astype(BF16), w_mlp1.astype(BF16), w_mlp2.astype(BF16)

    h = _rmsnorm(xf, norm1_g[0])
    for l in range(depth):
        gains = jnp.stack([a_qn_g[l].astype(F32) * scale, a_kn_g[l].astype(F32),
                           b_qn_g[l].astype(F32) * scale, b_kn_g[l].astype(F32)]).reshape(4, 1, HEAD_DIM)
        qk = _proj_heads(h, w_in, l, (0, 1, 3, 4), seg_cols=D, gains=gains)
        vv = _proj_heads(h, w_in, l, (2, 5), seg_cols=D)
        gates = _proj_heads(h, w_gate, l, (0, 1), seg_cols=D,
                            bias=b_gate[l].reshape(1, -1).astype(F32))
        out_c = _proj_conv(h, w_in, w_gate, l, b_gate[l], c_conv_w[l], seq=S,
                           bg_col=6 * D, cg_col=7 * D, xc_col=8 * D, gc_col=2 * D)

        lambda_init = 0.8 - 0.6 * math.exp(-0.3 * l)
        lam = (jnp.exp(jnp.sum(b_lq1[l].astype(F32) * b_lk1[l].astype(F32)))
               - jnp.exp(jnp.sum(b_lq2[l].astype(F32) * b_lk2[l].astype(F32))) + lambda_init).reshape(1)
        out_a, out_b = _attention(qk, vv, gates, _attn_a_bias_tile(a_rel_bias[l]), lam, b_subln_g[l],
                                  batch=B, seq=S, b_heads=b_heads, out_scale=1.0 - lambda_init)

        x1, h2 = _merge_out(xf, out_a, out_b, out_c, w_o_b, l, norm2_g[l])
        xf, h = _mlp(x1, h2, w1_b, w2_b, l, norm1_g[l + 1] if l + 1 < depth else None)
    return xf.reshape(B, S, D).astype(x.dtype)
```

```python
import functools
import math

import jax
import jax.numpy as jnp
from jax import lax
from jax.experimental import pallas as pl
from jax.experimental.pallas import tpu as pltpu

F32 = jnp.float32
BF16 = jnp.bfloat16

LANE = 128
F32_SUBLANES = 8
CHUNK = 64
HEAD_DIM = 128
EPS = 1e-6
A_LEFT_CHUNKS = 8
A_MAX_REL = 128
A_QBLOCK = 2 * CHUNK
A_BAND = (A_LEFT_CHUNKS + 2) * CHUNK
B_QBLOCK = 256
ATTN_B_HEADS = 1
PROJ_ROW_CHUNK = 256
CONV_COL_TILE = 256
MASKED = -1e30
LOG2E = math.log2(math.e)
VMEM_LIMIT = 56 * 1024 * 1024


def _params(*sem):
    return pltpu.CompilerParams(dimension_semantics=sem, vmem_limit_bytes=VMEM_LIMIT)


def _tile(n, pref):
    if n <= pref:
        return n
    t = pref
    while n % t:
        t -= LANE
    assert t > 0
    return t


def _rmsnorm_kernel(x_ref, g_ref, o_ref):
    x = x_ref[...]
    ms = jnp.mean(x * x, axis=-1, keepdims=True)
    o_ref[...] = (x * lax.rsqrt(ms + EPS) * g_ref[...]).astype(o_ref.dtype)


def _rmsnorm(x, g):
    T, D = x.shape
    tm = _tile(T, 512)
    return pl.pallas_call(
        _rmsnorm_kernel,
        grid=(T // tm,),
        in_specs=[pl.BlockSpec((tm, D), lambda i: (i, 0)),
                  pl.BlockSpec((1, D), lambda i: (0, 0))],
        out_specs=pl.BlockSpec((tm, D), lambda i: (i, 0)),
        out_shape=jax.ShapeDtypeStruct((T, D), BF16),
        compiler_params=_params("parallel"),
        name="rmsnorm",
    )(x, g.reshape(1, D).astype(F32))


def _cast_weights(w_refs, wb_ref):
    @pl.when(pl.program_id(1) == 0)
    def _():
        for n, w_ref in enumerate(w_refs):
            wb_ref[n] = w_ref[...].astype(wb_ref.dtype)


def _proj_heads_kernel(h_ref, w_ref, *rest, epilogue):
    p_ref, (o_ref, wb_ref) = rest[0], rest[-2:]
    _cast_weights([w_ref], wb_ref)
    tm = h_ref.shape[0]
    rc = min(tm, PROJ_ROW_CHUNK)
    for r in range(tm // rc):
        rows = slice(r * rc, (r + 1) * rc)
        acc = jnp.dot(h_ref[rows, :], wb_ref[0], preferred_element_type=F32)
        for g in range(o_ref.shape[0]):
            cols = slice(g * LANE, (g + 1) * LANE)
            a = acc[:, cols]
            if epilogue == "norm":
                ms = jnp.mean(a * a, axis=-1, keepdims=True)
                a = a * lax.rsqrt(ms + EPS) * p_ref[...]
            elif epilogue == "gate":
                a = jax.nn.sigmoid(a + p_ref[:, cols])
            o_ref[g, rows, :] = a.astype(o_ref.dtype)


def _proj_heads(h, w, layer, segs, *, seg_cols, gains=None, bias=None):
    T, D = h.shape
    tm = _tile(T, 2048)
    tn = _tile(seg_cols, 1024)
    tps = seg_cols // tn
    n_cols = len(segs) * seg_cols

    def col_block(j):
        k = j // tps
        seg = sum(jnp.where(k == n, s, 0) for n, s in enumerate(segs))
        return seg * tps + j % tps

    if gains is not None:
        epilogue, p, p_spec = "norm", gains, pl.BlockSpec((None, 1, LANE), lambda j, i: (j // tps, 0, 0))
    elif bias is not None:
        epilogue, p, p_spec = "gate", bias, pl.BlockSpec((1, tn), lambda j, i: (0, col_block(j)))
    else:
        epilogue, p, p_spec = "plain", None, None
    return pl.pallas_call(
        functools.partial(_proj_heads_kernel, epilogue=epilogue),
        grid=(n_cols // tn, T // tm),
        in_specs=[pl.BlockSpec((tm, D), lambda j, i: (i, 0)),
                  pl.BlockSpec((None, D, tn), lambda j, i: (layer, 0, col_block(j)))]
                 + ([] if p is None else [p_spec]),
        out_specs=pl.BlockSpec((tn // LANE, tm, LANE), lambda j, i: (j, i, 0)),
        out_shape=jax.ShapeDtypeStruct((n_cols // LANE, T, LANE), BF16),
        scratch_shapes=[pltpu.VMEM((1, D, tn), BF16)],
        compiler_params=_params("arbitrary", "arbitrary"),
        name="proj_heads_" + epilogue,
    )(h, w, *([] if p is None else [p]))


def _proj_conv_kernel(h_ref, wbg_ref, wcg_ref, wxc_ref, wgc_ref, b_ref, cw_ref, o_ref, wb_ref):
    _cast_weights([wbg_ref, wcg_ref, wxc_ref, wgc_ref], wb_ref)
    S = h_ref.shape[0]
    rc = min(S, PROJ_ROW_CHUNK)
    row = lax.broadcasted_iota(jnp.int32, (rc, 1), 0)
    tail = jnp.zeros((F32_SUBLANES, o_ref.shape[1]), F32)
    for r0 in range(0, S, rc):
        rows = slice(r0, r0 + rc)
        hc = h_ref[rows, :]
        bg, cg, xc, gc = [jnp.dot(hc, wb_ref[n], preferred_element_type=F32) for n in range(4)]
        u = cg * xc
        prev1, prev2 = tail[F32_SUBLANES - 1:, :], tail[F32_SUBLANES - 2:F32_SUBLANES - 1, :]
        u1 = jnp.where(row == 0, prev1, pltpu.roll(u, 1, 0))
        u2 = jnp.where(row == 0, prev2, jnp.where(row == 1, prev1, pltpu.roll(u, 2, 0)))
        conv = cw_ref[0:1, :] * u2 + cw_ref[1:2, :] * u1 + cw_ref[2:3, :] * u
        o_ref[rows, :] = (jax.nn.sigmoid(gc + b_ref[...]) * (bg * conv)).astype(o_ref.dtype)
        tail = u[rc - F32_SUBLANES:, :]


def _proj_conv(h, w_in, w_gate, layer, b_gate, conv_w, *, seq, bg_col, cg_col, xc_col, gc_col):
    T, D = h.shape
    tn = _tile(D, CONV_COL_TILE)

    def w_spec(col0):
        return pl.BlockSpec((None, D, tn), lambda j, i: (layer, 0, col0 // tn + j))

    return pl.pallas_call(
        _proj_conv_kernel,
        grid=(D // tn, T // seq),
        in_specs=[pl.BlockSpec((seq, D), lambda j, i: (i, 0)),
                  w_spec(bg_col), w_spec(cg_col), w_spec(xc_col), w_spec(gc_col),
                  pl.BlockSpec((1, tn), lambda j, i: (0, gc_col // tn + j)),
                  pl.BlockSpec((3, tn), lambda j, i: (0, j))],
        out_specs=pl.BlockSpec((seq, tn), lambda j, i: (i, j)),
        out_shape=jax.ShapeDtypeStruct((T, D), BF16),
        scratch_shapes=[pltpu.VMEM((4, D, tn), BF16)],
        compiler_params=_params("arbitrary", "arbitrary"),
        name="proj_conv",
    )(h, w_in, w_in, w_in, w_gate, b_gate.reshape(1, -1).astype(F32), conv_w.astype(F32))


def _attn_a_bias_tile(rel_bias):
    H = rel_bias.shape[0]
    table = rel_bias.astype(F32)
    n_far = A_LEFT_CHUNKS * CHUNK - A_MAX_REL + CHUNK - 1
    far = jnp.broadcast_to(table[:, -1:], (H, n_far))
    near = table[:, :A_MAX_REL - CHUNK:-1]
    f = jnp.concatenate([far, near, jnp.zeros((H, 1), F32)], axis=1)
    L = f.shape[1] - 1
    windows = jnp.tile(f, (1, CHUNK))[:, :CHUNK * L].reshape(H, CHUNK, L)
    t = windows[:, :, CHUNK - 1:]
    masked = jnp.full((H, CHUNK, CHUNK), MASKED, F32)
    t = t * LOG2E
    return jnp.concatenate([jnp.concatenate([t, masked], axis=2),
                            jnp.concatenate([masked, t], axis=2)], axis=1)


def _attn_a_steps(q_ref, k_ref, v_ref, bias_ref, gate_ref, o_ref, vx_ref):
    S = q_ref.shape[0]
    n_blocks = S // A_QBLOCK
    vx_ref[:, :HEAD_DIM] = v_ref[...]
    vx_ref[:, HEAD_DIM:] = jnp.ones((S, HEAD_DIM), vx_ref.dtype)

    def band(blk):
        k1 = (blk + 1) * A_QBLOCK
        return max(0, k1 - A_BAND), k1

    def scores(blk):
        k0, k1 = band(blk)
        b0 = A_BAND - (k1 - k0)
        s = lax.dot_general(q_ref[blk * A_QBLOCK:k1, :], k_ref[k0:k1, :], (((1,), (1,)), ((), ())),
                            preferred_element_type=F32)
        return s + bias_ref[:, b0:]

    s_next = scores(0)
    for blk in range(n_blocks):
        s = s_next
        if blk + 1 < n_blocks:
            s_next = scores(blk + 1)
        k0, k1 = band(blk)
        e = jnp.exp2(s - jnp.max(s, axis=-1, keepdims=True))
        ox = jnp.dot(e.astype(BF16), vx_ref[k0:k1, :], preferred_element_type=F32)
        rows = slice(blk * A_QBLOCK, k1)
        o_ref[rows, :] = (ox[:, :HEAD_DIM] / ox[:, HEAD_DIM:]).astype(o_ref.dtype) * gate_ref[rows, :]
        yield


def _attn_b_steps(lam_ref, q_ref, k_ref, v_ref, g_ref, gate_ref, o_ref, *, out_scale):
    S = q_ref.shape[1]
    lam = lam_ref[0]
    v = jnp.concatenate([v_ref[0], v_ref[1]], axis=-1)
    chunk = lax.broadcasted_iota(jnp.int32, (B_QBLOCK, HEAD_DIM), 0) // CHUNK
    lane = lax.broadcasted_iota(jnp.int32, (B_QBLOCK, HEAD_DIM), 1)
    q_hot = jnp.where(lane == chunk, 1.0, 0.0).astype(q_ref.dtype)
    k_hot = jnp.where((lane < B_QBLOCK // CHUNK) & (chunk > lane), MASKED, 0.0).astype(k_ref.dtype)
    gs = g_ref[...] * out_scale
    nt = (((1,), (1,)), ((), ()))

    def scores(blk):
        q0, q1 = blk * B_QBLOCK, (blk + 1) * B_QBLOCK
        out = []
        for m in range(2):
            q = q_ref[m, q0:q1, :]
            sd = lax.dot_general(jnp.concatenate([q, q_hot], axis=1),
                                 jnp.concatenate([k_ref[m, q0:q1, :], k_hot], axis=1), nt,
                                 preferred_element_type=F32)
            sl = lax.dot_general(q, k_ref[m, 0:q0, :], nt, preferred_element_type=F32) if blk else None
            out.append((sl, sd))
        return out

    def softmax_parts(sl, sd):
        mx = jnp.max(sd, axis=-1, keepdims=True)
        el = None
        if sl is not None:
            mx = jnp.maximum(mx, jnp.max(sl, axis=-1, keepdims=True))
            el = jnp.exp2(sl - mx)
        ed = jnp.exp2(sd - mx)
        l = jnp.sum(ed, axis=-1, keepdims=True)
        if sl is not None:
            l = l + jnp.sum(el, axis=-1, keepdims=True)
        return el, ed, l

    n_blocks = S // B_QBLOCK
    s_next = scores(0)
    for blk in range(n_blocks):
        q0, q1 = blk * B_QBLOCK, (blk + 1) * B_QBLOCK
        (sl1, sd1), (sl2, sd2) = s_next
        if blk + 1 < n_blocks:
            s_next = scores(blk + 1)
        yield
        el1, ed1, l1 = softmax_parts(sl1, sd1)
        el2, ed2, l2 = softmax_parts(sl2, sd2)
        r = lam * l1 / l2
        ob = jnp.dot((ed1 - ed2 * r).astype(BF16), v[q0:q1, :], preferred_element_type=F32)
        if blk:
            ob = ob + jnp.dot((el1 - el2 * r).astype(BF16), v[0:q0, :], preferred_element_type=F32)
        inv_l1 = 1.0 / l1
        ms = jnp.mean(ob * ob, axis=-1, keepdims=True) * (inv_l1 * inv_l1)
        y = (ob * (inv_l1 * lax.rsqrt(ms + EPS)) * gs).astype(o_ref.dtype)
        for half in range(2):
            cols = slice(half * HEAD_DIM, (half + 1) * HEAD_DIM)
            o_ref[q0:q1, cols] = y[:, cols] * gate_ref[half, q0:q1, :]
        yield


def _attn_kernel(lam_ref, qa_ref, ka_ref, va_ref, bias_ref, ga_ref, qb_ref, kb_ref, vb_ref, g_ref, gb_ref,
                 oa_ref, ob_ref, vx_ref, *, out_scale):
    S = qa_ref.shape[1]
    for i in range(ob_ref.shape[0]):
        pair = pl.ds(2 * i, 2)
        for _ in _attn_b_steps(lam_ref, qb_ref.at[pair], kb_ref.at[pair], vb_ref.at[pair], g_ref, gb_ref.at[pair],
                               ob_ref.at[i], out_scale=out_scale):
            pass
    a_progs = [_attn_a_steps(qa_ref.at[i], ka_ref.at[i], va_ref.at[i], bias_ref.at[i], ga_ref.at[i],
                             oa_ref.at[i], vx_ref.at[i]) for i in range(qa_ref.shape[0])]
    for _ in range(S // A_QBLOCK):
        for prog in a_progs:
            next(prog)


def _attention(qk, vv, gates, bias_tile, lam, subln_g, *, batch, seq, b_heads, out_scale):
    T = qk.shape[1]
    nb = math.gcd(ATTN_B_HEADS, b_heads)
    steps = b_heads // nb

    def groups(seg):
        return pl.BlockSpec((2 * nb, seq, HEAD_DIM), lambda b, h: (seg * steps + h, b, 0))

    return pl.pallas_call(
        functools.partial(_attn_kernel, out_scale=out_scale),
        grid=(batch, steps),
        in_specs=[pl.BlockSpec(memory_space=pltpu.SMEM),
                  groups(0), groups(1), groups(0),
                  pl.BlockSpec((2 * nb, A_QBLOCK, A_BAND), lambda b, h: (h, 0, 0)), groups(0),
                  groups(2), groups(3), groups(1),
                  pl.BlockSpec((1, 2 * HEAD_DIM), lambda b, h: (0, 0)), groups(1)],
        out_specs=[pl.BlockSpec((2 * nb, seq, HEAD_DIM), lambda b, h: (h, b, 0)),
                   pl.BlockSpec((nb, seq, 2 * HEAD_DIM), lambda b, h: (h, b, 0))],
        out_shape=[jax.ShapeDtypeStruct((2 * b_heads, T, HEAD_DIM), BF16),
                   jax.ShapeDtypeStruct((b_heads, T, 2 * HEAD_DIM), BF16)],
        scratch_shapes=[pltpu.VMEM((2 * nb, seq, 2 * HEAD_DIM), BF16)],
        compiler_params=_params("parallel", "parallel"),
        name="attention",
    )(lam, qk, qk, vv, bias_tile, gates, qk, qk, vv, subln_g.reshape(1, 2 * HEAD_DIM).astype(F32), gates)


def _merge_out_kernel(x_ref, oa_ref, ob_ref, oc_ref, wo_ref, g2_ref, x1_ref, h2_ref, m_ref):
    D = x_ref.shape[1]
    for g in range(D // LANE):
        sl = slice(g * LANE, (g + 1) * LANE)
        half = slice((g % 2) * LANE, (g % 2 + 1) * LANE)
        m_ref[:, sl] = oa_ref[g] + ob_ref[g // 2, :, half] + oc_ref[:, sl]
    x1 = x_ref[...] + jnp.dot(m_ref[...], wo_ref[...], preferred_element_type=F32)
    x1_ref[...] = x1
    ms = jnp.mean(x1 * x1, axis=-1, keepdims=True)
    h2_ref[...] = (x1 * lax.rsqrt(ms + EPS) * g2_ref[...]).astype(h2_ref.dtype)


def _merge_out(x, out_a, out_b, out_c, w_o, layer, g2):
    T, D = x.shape
    tm = _tile(T, 512)
    row = lambda i: (i, 0)
    once = pl.Buffered(1)
    return pl.pallas_call(
        _merge_out_kernel,
        grid=(T // tm,),
        in_specs=[pl.BlockSpec((tm, D), row),
                  pl.BlockSpec((out_a.shape[0], tm, HEAD_DIM), lambda i: (0, i, 0)),
                  pl.BlockSpec((out_b.shape[0], tm, 2 * HEAD_DIM), lambda i: (0, i, 0)),
                  pl.BlockSpec((tm, D), row),
                  pl.BlockSpec((None, D, D), lambda i: (layer, 0, 0), pipeline_mode=once),
                  pl.BlockSpec((1, D), lambda i: (0, 0), pipeline_mode=once)],
        out_specs=[pl.BlockSpec((tm, D), row), pl.BlockSpec((tm, D), row)],
        out_shape=[jax.ShapeDtypeStruct((T, D), F32), jax.ShapeDtypeStruct((T, D), BF16)],
        scratch_shapes=[pltpu.VMEM((tm, D), BF16)],
        compiler_params=_params("parallel"),
        name="merge_out",
    )(x, out_a, out_b, out_c, w_o, g2.reshape(1, D).astype(F32))


def _mlp_kernel(x1_ref, h2_ref, w1_ref, w2_ref, *rest, with_next):
    x2_ref = rest[1] if with_next else rest[0]
    f = pl.program_id(1)
    tm = h2_ref.shape[0]
    rc = min(tm, PROJ_ROW_CHUNK)
    chunks = [slice(r * rc, (r + 1) * rc) for r in range(tm // rc)]

    @pl.when(f == 0)
    def _():
        x2_ref[...] = x1_ref[...]

    acts = [jnp.dot(h2_ref[rows, :], w1_ref[...], preferred_element_type=F32) for rows in chunks]
    for rows, a in zip(chunks, acts):
        a = jnp.square(jnp.maximum(a, 0.0)).astype(BF16)
        x2_ref[rows, :] += jnp.dot(a, w2_ref[...], preferred_element_type=F32)

    if with_next:
        @pl.when(f == pl.num_programs(1) - 1)
        def _():
            x2 = x2_ref[...]
            ms = jnp.mean(x2 * x2, axis=-1, keepdims=True)
            rest[2][...] = (x2 * lax.rsqrt(ms + EPS) * rest[0][...]).astype(BF16)


def _mlp(x1, h2, w1, w2, layer, g_next):
    T, D = x1.shape
    FF = w1.shape[2]
    tm = _tile(T, 512)
    tf = _tile(FF, 1024)
    with_next = g_next is not None
    row = lambda i, f: (i, 0)
    in_specs = [pl.BlockSpec((tm, D), row),
                pl.BlockSpec((tm, D), row),
                pl.BlockSpec((None, D, tf), lambda i, f: (layer, 0, f)),
                pl.BlockSpec((None, tf, D), lambda i, f: (layer, f, 0))]
    args = [x1, h2, w1, w2]
    out_specs = [pl.BlockSpec((tm, D), row)]
    out_shape = [jax.ShapeDtypeStruct((T, D), F32)]
    if with_next:
        in_specs.append(pl.BlockSpec((1, D), lambda i, f: (0, 0)))
        args.append(g_next.reshape(1, D).astype(F32))
        out_specs.append(pl.BlockSpec((tm, D), row))
        out_shape.append(jax.ShapeDtypeStruct((T, D), BF16))
    out = pl.pallas_call(
        functools.partial(_mlp_kernel, with_next=with_next),
        grid=(T // tm, FF // tf),
        in_specs=in_specs,
        out_specs=out_specs,
        out_shape=out_shape,
        compiler_params=_params("parallel", "arbitrary"),
        name="mlp",
    )(*args)
    return (out[0], out[1]) if with_next else (out[0], None)


def kernel(x, norm1_g, w_in, w_gate, b_gate, a_qn_g, a_kn_g, a_rel_bias, b_qn_g, b_kn_g,
           b_lq1, b_lk1, b_lq2, b_lk2, b_subln_g, c_conv_w, w_o, norm2_g, w_mlp1, w_mlp2):
    B, S, D = x.shape
    depth = w_in.shape[0]
    T = B * S
    b_heads = D // (2 * HEAD_DIM)
    assert S % B_QBLOCK == 0 and D % (2 * HEAD_DIM) == 0 and w_in.shape[2] == 9 * D
    scale = HEAD_DIM ** -0.5 * LOG2E

    xf = x.reshape(T, D).astype(F32)
    w_in, w_gate = w_in.astype(F32), w_gate.astype(F32)
    w_o_b, w1_b, w2_b = w_o.astype(BF16), w_mlp1.astype(BF16), w_mlp2.astype(BF16)

    h = _rmsnorm(xf, norm1_g[0])
    for l in range(depth):
        gains = jnp.stack([a_qn_g[l].astype(F32) * scale, a_kn_g[l].astype(F32),
                           b_qn_g[l].astype(F32) * scale, b_kn_g[l].astype(F32)]).reshape(4, 1, HEAD_DIM)
        qk = _proj_heads(h, w_in, l, (0, 1, 3, 4), seg_cols=D, gains=gains)
        vv = _proj_heads(h, w_in, l, (2, 5), seg_cols=D)
        gates = _proj_heads(h, w_gate, l, (0, 1), seg_cols=D,
                            bias=b_gate[l].reshape(1, -1).astype(F32))
        out_c = _proj_conv(h, w_in, w_gate, l, b_gate[l], c_conv_w[l], seq=S,
                           bg_col=6 * D, cg_col=7 * D, xc_col=8 * D, gc_col=2 * D)

        lambda_init = 0.8 - 0.6 * math.exp(-0.3 * l)
        lam = (jnp.exp(jnp.sum(b_lq1[l].astype(F32) * b_lk1[l].astype(F32)))
               - jnp.exp(jnp.sum(b_lq2[l].astype(F32) * b_lk2[l].astype(F32))) + lambda_init).reshape(1)
        out_a, out_b = _attention(qk, vv, gates, _attn_a_bias_tile(a_rel_bias[l]), lam, b_subln_g[l],
                                  batch=B, seq=S, b_heads=b_heads, out_scale=1.0 - lambda_init)

        x1, h2 = _merge_out(xf, out_a, out_b, out_c, w_o_b, l, norm2_g[l])
        xf, h = _mlp(x1, h2, w1_b, w2_b, l, norm1_g[l + 1] if l + 1 < depth else None)
    return xf.reshape(B, S, D).astype(x.dtype)
```

```python
import functools
import math

import jax
import jax.numpy as jnp
from jax import lax
from jax.experimental import pallas as pl
from jax.experimental.pallas import tpu as pltpu

F32 = jnp.float32
BF16 = jnp.bfloat16

LANE = 128
F32_SUBLANES = 8
CHUNK = 64
HEAD_DIM = 128
EPS = 1e-6
A_LEFT_CHUNKS = 8
A_MAX_REL = 128
A_QBLOCK = 2 * CHUNK
A_BAND = (A_LEFT_CHUNKS + 2) * CHUNK
B_QBLOCK = 256
ATTN_B_HEADS = 1
PROJ_ROW_CHUNK = 256
CONV_COL_TILE = 256
MASKED = -1e30
LOG2E = math.log2(math.e)
VMEM_LIMIT = 56 * 1024 * 1024


def _params(*sem):
    return pltpu.CompilerParams(dimension_semantics=sem, vmem_limit_bytes=VMEM_LIMIT)


def _tile(n, pref):
    if n <= pref:
        return n
    t = pref
    while n % t:
        t -= LANE
    assert t > 0
    return t


def _rmsnorm_kernel(x_ref, g_ref, o_ref):
    x = x_ref[...]
    ms = jnp.mean(x * x, axis=-1, keepdims=True)
    o_ref[...] = (x * lax.rsqrt(ms + EPS) * g_ref[...]).astype(o_ref.dtype)


def _rmsnorm(x, g):
    T, D = x.shape
    tm = _tile(T, 512)
    return pl.pallas_call(
        _rmsnorm_kernel,
        grid=(T // tm,),
        in_specs=[pl.BlockSpec((tm, D), lambda i: (i, 0)),
                  pl.BlockSpec((1, D), lambda i: (0, 0))],
        out_specs=pl.BlockSpec((tm, D), lambda i: (i, 0)),
        out_shape=jax.ShapeDtypeStruct((T, D), BF16),
        compiler_params=_params("parallel"),
        name="rmsnorm",
    )(x, g.reshape(1, D).astype(F32))


def _cast_weights(w_refs, wb_ref):
    @pl.when(pl.program_id(1) == 0)
    def _():
        for n, w_ref in enumerate(w_refs):
            wb_ref[n] = w_ref[...].astype(wb_ref.dtype)


def _proj_heads_kernel(h_ref, w_ref, *rest, epilogue):
    p_ref, (o_ref, wb_ref) = rest[0], rest[-2:]
    _cast_weights([w_ref], wb_ref)
    tm = h_ref.shape[0]
    rc = min(tm, PROJ_ROW_CHUNK)
    for r in range(tm // rc):
        rows = slice(r * rc, (r + 1) * rc)
        acc = jnp.dot(h_ref[rows, :], wb_ref[0], preferred_element_type=F32)
        for g in range(o_ref.shape[0]):
            cols = slice(g * LANE, (g + 1) * LANE)
            a = acc[:, cols]
            if epilogue == "norm":
                ms = jnp.mean(a * a, axis=-1, keepdims=True)
                a = a * lax.rsqrt(ms + EPS) * p_ref[...]
            elif epilogue == "gate":
                a = jax.nn.sigmoid(a + p_ref[:, cols])
            o_ref[g, rows, :] = a.astype(o_ref.dtype)


def _proj_heads(h, w, layer, segs, *, seg_cols, gains=None, bias=None):
    T, D = h.shape
    tm = _tile(T, 2048)
    tn = _tile(seg_cols, 1024)
    tps = seg_cols // tn
    n_cols = len(segs) * seg_cols

    def col_block(j):
        k = j // tps
        seg = sum(jnp.where(k == n, s, 0) for n, s in enumerate(segs))
        return seg * tps + j % tps

    if gains is not None:
        epilogue, p, p_spec = "norm", gains, pl.BlockSpec((None, 1, LANE), lambda j, i: (j // tps, 0, 0))
    elif bias is not None:
        epilogue, p, p_spec = "gate", bias, pl.BlockSpec((1, tn), lambda j, i: (0, col_block(j)))
    else:
        epilogue, p, p_spec = "plain", None, None
    return pl.pallas_call(
        functools.partial(_proj_heads_kernel, epilogue=epilogue),
        grid=(n_cols // tn, T // tm),
        in_specs=[pl.BlockSpec((tm, D), lambda j, i: (i, 0)),
                  pl.BlockSpec((None, D, tn), lambda j, i: (layer, 0, col_block(j)))]
                 + ([] if p is None else [p_spec]),
        out_specs=pl.BlockSpec((tn // LANE, tm, LANE), lambda j, i: (j, i, 0)),
        out_shape=jax.ShapeDtypeStruct((n_cols // LANE, T, LANE), BF16),
        scratch_shapes=[pltpu.VMEM((1, D, tn), BF16)],
        compiler_params=_params("arbitrary", "arbitrary"),
        name="proj_heads_" + epilogue,
    )(h, w, *([] if p is None else [p]))


def _proj_conv_kernel(h_ref, wbg_ref, wcg_ref, wxc_ref, wgc_ref, b_ref, cw_ref, o_ref, wb_ref):
    _cast_weights([wbg_ref, wcg_ref, wxc_ref, wgc_ref], wb_ref)
    S = h_ref.shape[0]
    rc = min(S, PROJ_ROW_CHUNK)
    row = lax.broadcasted_iota(jnp.int32, (rc, 1), 0)
    tail = jnp.zeros((F32_SUBLANES, o_ref.shape[1]), F32)
    for r0 in range(0, S, rc):
        rows = slice(r0, r0 + rc)
        hc = h_ref[rows, :]
        bg, cg, xc, gc = [jnp.dot(hc, wb_ref[n], preferred_element_type=F32) for n in range(4)]
        u = cg * xc
        prev1, prev2 = tail[F32_SUBLANES - 1:, :], tail[F32_SUBLANES - 2:F32_SUBLANES - 1, :]
        u1 = jnp.where(row == 0, prev1, pltpu.roll(u, 1, 0))
        u2 = jnp.where(row == 0, prev2, jnp.where(row == 1, prev1, pltpu.roll(u, 2, 0)))
        conv = cw_ref[0:1, :] * u2 + cw_ref[1:2, :] * u1 + cw_ref[2:3, :] * u
        o_ref[rows, :] = (jax.nn.sigmoid(gc + b_ref[...]) * (bg * conv)).astype(o_ref.dtype)
        tail = u[rc - F32_SUBLANES:, :]


def _proj_conv(h, w_in, w_gate, layer, b_gate, conv_w, *, seq, bg_col, cg_col, xc_col, gc_col):
    T, D = h.shape
    tn = _tile(D, CONV_COL_TILE)

    def w_spec(col0):
        return pl.BlockSpec((None, D, tn), lambda j, i: (layer, 0, col0 // tn + j))

    return pl.pallas_call(
        _proj_conv_kernel,
        grid=(D // tn, T // seq),
        in_specs=[pl.BlockSpec((seq, D), lambda j, i: (i, 0)),
                  w_spec(bg_col), w_spec(cg_col), w_spec(xc_col), w_spec(gc_col),
                  pl.BlockSpec((1, tn), lambda j, i: (0, gc_col // tn + j)),
                  pl.BlockSpec((3, tn), lambda j, i: (0, j))],
        out_specs=pl.BlockSpec((seq, tn), lambda j, i: (i, j)),
        out_shape=jax.ShapeDtypeStruct((T, D), BF16),
        scratch_shapes=[pltpu.VMEM((4, D, tn), BF16)],
        compiler_params=_params("arbitrary", "arbitrary"),
        name="proj_conv",
    )(h, w_in, w_in, w_in, w_gate, b_gate.reshape(1, -1).astype(F32), conv_w.astype(F32))


def _attn_a_bias_tile(rel_bias):
    H = rel_bias.shape[0]
    table = rel_bias.astype(F32)
    n_far = A_LEFT_CHUNKS * CHUNK - A_MAX_REL + CHUNK - 1
    far = jnp.broadcast_to(table[:, -1:], (H, n_far))
    near = table[:, :A_MAX_REL - CHUNK:-1]
    f = jnp.concatenate([far, near, jnp.zeros((H, 1), F32)], axis=1)
    L = f.shape[1] - 1
    windows = jnp.tile(f, (1, CHUNK))[:, :CHUNK * L].reshape(H, CHUNK, L)
    t = windows[:, :, CHUNK - 1:]
    masked = jnp.full((H, CHUNK, CHUNK), MASKED, F32)
    t = t * LOG2E
    return jnp.concatenate([jnp.concatenate([t, masked], axis=2),
                            jnp.concatenate([masked, t], axis=2)], axis=1)


def _attn_a_steps(q_ref, k_ref, v_ref, bias_ref, gate_ref, o_ref, cols, vx_ref):
    S = q_ref.shape[0]
    n_blocks = S // A_QBLOCK
    vx_ref[:, :HEAD_DIM] = v_ref[...]
    vx_ref[:, HEAD_DIM:] = jnp.ones((S, HEAD_DIM), vx_ref.dtype)

    def band(blk):
        k1 = (blk + 1) * A_QBLOCK
        return max(0, k1 - A_BAND), k1

    def scores(blk):
        k0, k1 = band(blk)
        b0 = A_BAND - (k1 - k0)
        s = lax.dot_general(q_ref[blk * A_QBLOCK:k1, :], k_ref[k0:k1, :], (((1,), (1,)), ((), ())),
                            preferred_element_type=F32)
        return s + bias_ref[:, b0:]

    s_next = scores(0)
    for blk in range(n_blocks):
        s = s_next
        if blk + 1 < n_blocks:
            s_next = scores(blk + 1)
        k0, k1 = band(blk)
        e = jnp.exp2(s - jnp.max(s, axis=-1, keepdims=True))
        ox = jnp.dot(e.astype(BF16), vx_ref[k0:k1, :], preferred_element_type=F32)
        rows = slice(blk * A_QBLOCK, k1)
        o_ref[rows, cols] += (ox[:, :HEAD_DIM] / ox[:, HEAD_DIM:]).astype(o_ref.dtype) * gate_ref[rows, :]
        yield


def _attn_b_steps(lam_ref, q_ref, k_ref, v_ref, g_ref, gate_ref, c_ref, o_ref, col0, *, out_scale):
    S = q_ref.shape[1]
    lam = lam_ref[0]
    v = jnp.concatenate([v_ref[0], v_ref[1]], axis=-1)
    chunk = lax.broadcasted_iota(jnp.int32, (B_QBLOCK, HEAD_DIM), 0) // CHUNK
    lane = lax.broadcasted_iota(jnp.int32, (B_QBLOCK, HEAD_DIM), 1)
    q_hot = jnp.where(lane == chunk, 1.0, 0.0).astype(q_ref.dtype)
    k_hot = jnp.where((lane < B_QBLOCK // CHUNK) & (chunk > lane), MASKED, 0.0).astype(k_ref.dtype)
    gs = g_ref[...] * out_scale
    nt = (((1,), (1,)), ((), ()))

    def scores(blk):
        q0, q1 = blk * B_QBLOCK, (blk + 1) * B_QBLOCK
        out = []
        for m in range(2):
            q = q_ref[m, q0:q1, :]
            sd = lax.dot_general(jnp.concatenate([q, q_hot], axis=1),
                                 jnp.concatenate([k_ref[m, q0:q1, :], k_hot], axis=1), nt,
                                 preferred_element_type=F32)
            sl = lax.dot_general(q, k_ref[m, 0:q0, :], nt, preferred_element_type=F32) if blk else None
            out.append((sl, sd))
        return out

    def softmax_parts(sl, sd):
        mx = jnp.max(sd, axis=-1, keepdims=True)
        el = None
        if sl is not None:
            mx = jnp.maximum(mx, jnp.max(sl, axis=-1, keepdims=True))
            el = jnp.exp2(sl - mx)
        ed = jnp.exp2(sd - mx)
        l = jnp.sum(ed, axis=-1, keepdims=True)
        if sl is not None:
            l = l + jnp.sum(el, axis=-1, keepdims=True)
        return el, ed, l

    n_blocks = S // B_QBLOCK
    s_next = scores(0)
    for blk in range(n_blocks):
        q0, q1 = blk * B_QBLOCK, (blk + 1) * B_QBLOCK
        (sl1, sd1), (sl2, sd2) = s_next
        if blk + 1 < n_blocks:
            s_next = scores(blk + 1)
        yield
        el1, ed1, l1 = softmax_parts(sl1, sd1)
        el2, ed2, l2 = softmax_parts(sl2, sd2)
        r = lam * l1 / l2
        ob = jnp.dot((ed1 - ed2 * r).astype(BF16), v[q0:q1, :], preferred_element_type=F32)
        if blk:
            ob = ob + jnp.dot((el1 - el2 * r).astype(BF16), v[0:q0, :], preferred_element_type=F32)
        inv_l1 = 1.0 / l1
        ms = jnp.mean(ob * ob, axis=-1, keepdims=True) * (inv_l1 * inv_l1)
        y = (ob * (inv_l1 * lax.rsqrt(ms + EPS)) * gs).astype(o_ref.dtype)
        for half in range(2):
            cols = slice(half * HEAD_DIM, (half + 1) * HEAD_DIM)
            out = slice(col0 + half * HEAD_DIM, col0 + (half + 1) * HEAD_DIM)
            o_ref[q0:q1, out] = y[:, cols] * gate_ref[half, q0:q1, :] + c_ref[q0:q1, out]
        yield


def _attn_kernel(lam_ref, qa_ref, ka_ref, va_ref, bias_ref, ga_ref, qb_ref, kb_ref, vb_ref, g_ref, gb_ref, c_ref,
                 o_ref, vx_ref, *, out_scale):
    S = qa_ref.shape[1]
    for i in range(qb_ref.shape[0] // 2):
        pair = pl.ds(2 * i, 2)
        for _ in _attn_b_steps(lam_ref, qb_ref.at[pair], kb_ref.at[pair], vb_ref.at[pair], g_ref, gb_ref.at[pair],
                               c_ref, o_ref, 2 * HEAD_DIM * i, out_scale=out_scale):
            pass
    a_progs = [_attn_a_steps(qa_ref.at[i], ka_ref.at[i], va_ref.at[i], bias_ref.at[i], ga_ref.at[i],
                             o_ref, slice(i * HEAD_DIM, (i + 1) * HEAD_DIM), vx_ref.at[i])
               for i in range(qa_ref.shape[0])]
    for _ in range(S // A_QBLOCK):
        for prog in a_progs:
            next(prog)


def _attention(qk, vv, gates, out_c, bias_tile, lam, subln_g, *, batch, seq, b_heads, out_scale):
    T = qk.shape[1]
    nb = math.gcd(ATTN_B_HEADS, b_heads)
    steps = b_heads // nb

    def groups(seg):
        return pl.BlockSpec((2 * nb, seq, HEAD_DIM), lambda b, h: (seg * steps + h, b, 0))

    cols = pl.BlockSpec((seq, 2 * HEAD_DIM * nb), lambda b, h: (b, h))
    return pl.pallas_call(
        functools.partial(_attn_kernel, out_scale=out_scale),
        grid=(batch, steps),
        in_specs=[pl.BlockSpec(memory_space=pltpu.SMEM),
                  groups(0), groups(1), groups(0),
                  pl.BlockSpec((2 * nb, A_QBLOCK, A_BAND), lambda b, h: (h, 0, 0)), groups(0),
                  groups(2), groups(3), groups(1),
                  pl.BlockSpec((1, 2 * HEAD_DIM), lambda b, h: (0, 0)), groups(1), cols],
        out_specs=cols,
        out_shape=jax.ShapeDtypeStruct(out_c.shape, BF16),
        scratch_shapes=[pltpu.VMEM((2 * nb, seq, 2 * HEAD_DIM), BF16)],
        compiler_params=_params("parallel", "parallel"),
        name="attention",
    )(lam, qk, qk, vv, bias_tile, gates, qk, qk, vv, subln_g.reshape(1, 2 * HEAD_DIM).astype(F32), gates, out_c)


def _merge_out_kernel(x_ref, m_ref, wo_ref, g2_ref, x1_ref, h2_ref):
    x1 = x_ref[...] + jnp.dot(m_ref[...], wo_ref[...], preferred_element_type=F32)
    x1_ref[...] = x1
    ms = jnp.mean(x1 * x1, axis=-1, keepdims=True)
    h2_ref[...] = (x1 * lax.rsqrt(ms + EPS) * g2_ref[...]).astype(h2_ref.dtype)


def _merge_out(x, m, w_o, layer, g2):
    T, D = x.shape
    tm = _tile(T, 512)
    row = lambda i: (i, 0)
    once = pl.Buffered(1)
    return pl.pallas_call(
        _merge_out_kernel,
        grid=(T // tm,),
        in_specs=[pl.BlockSpec((tm, D), row),
                  pl.BlockSpec((tm, D), row),
                  pl.BlockSpec((None, D, D), lambda i: (layer, 0, 0), pipeline_mode=once),
                  pl.BlockSpec((1, D), lambda i: (0, 0), pipeline_mode=once)],
        out_specs=[pl.BlockSpec((tm, D), row), pl.BlockSpec((tm, D), row)],
        out_shape=[jax.ShapeDtypeStruct((T, D), F32), jax.ShapeDtypeStruct((T, D), BF16)],
        compiler_params=_params("parallel"),
        name="merge_out",
    )(x, m, w_o, g2.reshape(1, D).astype(F32))


def _mlp_kernel(x1_ref, h2_ref, w1_ref, w2_ref, *rest, with_next):
    x2_ref = rest[1] if with_next else rest[0]
    f = pl.program_id(1)
    tm = h2_ref.shape[0]
    rc = min(tm, PROJ_ROW_CHUNK)
    chunks = [slice(r * rc, (r + 1) * rc) for r in range(tm // rc)]

    @pl.when(f == 0)
    def _():
        x2_ref[...] = x1_ref[...]

    acts = [jnp.dot(h2_ref[rows, :], w1_ref[...], preferred_element_type=F32) for rows in chunks]
    for rows, a in zip(chunks, acts):
        a = jnp.square(jnp.maximum(a, 0.0)).astype(BF16)
        x2_ref[rows, :] += jnp.dot(a, w2_ref[...], preferred_element_type=F32)

    if with_next:
        @pl.when(f == pl.num_programs(1) - 1)
        def _():
            x2 = x2_ref[...]
            ms = jnp.mean(x2 * x2, axis=-1, keepdims=True)
            rest[2][...] = (x2 * lax.rsqrt(ms + EPS) * rest[0][...]).astype(BF16)


def _mlp(x1, h2, w1, w2, layer, g_next):
    T, D = x1.shape
    FF = w1.shape[2]
    tm = _tile(T, 512)
    tf = _tile(FF, 1024)
    with_next = g_next is not None
    row = lambda i, f: (i, 0)
    in_specs = [pl.BlockSpec((tm, D), row),
                pl.BlockSpec((tm, D), row),
                pl.BlockSpec((None, D, tf), lambda i, f: (layer, 0, f)),
                pl.BlockSpec((None, tf, D), lambda i, f: (layer, f, 0))]
    args = [x1, h2, w1, w2]
    out_specs = [pl.BlockSpec((tm, D), row)]
    out_shape = [jax.ShapeDtypeStruct((T, D), F32)]
    if with_next:
        in_specs.append(pl.BlockSpec((1, D), lambda i, f: (0, 0)))
        args.append(g_next.reshape(1, D).astype(F32))
        out_specs.append(pl.BlockSpec((tm, D), row))
        out_shape.append(jax.ShapeDtypeStruct((T, D), BF16))
    out = pl.pallas_call(
        functools.partial(_mlp_kernel, with_next=with_next),
        grid=(T // tm, FF // tf),
        in_specs=in_specs,
        out_specs=out_specs,
        out_shape=out_shape,
        compiler_params=_params("parallel", "arbitrary"),
        name="mlp",
    )(*args)
    return (out[0], out[1]) if with_next else (out[0], None)


def kernel(x, norm1_g, w_in, w_gate, b_gate, a_qn_g, a_kn_g, a_rel_bias, b_qn_g, b_kn_g,
           b_lq1, b_lk1, b_lq2, b_lk2, b_subln_g, c_conv_w, w_o, norm2_g, w_mlp1, w_mlp2):
    B, S, D = x.shape
    depth = w_in.shape[0]
    T = B * S
    b_heads = D // (2 * HEAD_DIM)
    assert S % B_QBLOCK == 0 and D % (2 * HEAD_DIM) == 0 and w_in.shape[2] == 9 * D
    scale = HEAD_DIM ** -0.5 * LOG2E

    xf = x.reshape(T, D).astype(F32)
    w_in, w_gate = w_in.astype(F32), w_gate.astype(F32)
    w_o_b, w1_b, w2_b = w_o.astype(BF16), w_mlp1.astype(BF16), w_mlp2.astype(BF16)

    h = _rmsnorm(xf, norm1_g[0])
    for l in range(depth):
        gains = jnp.stack([a_qn_g[l].astype(F32) * scale, a_kn_g[l].astype(F32),
                           b_qn_g[l].astype(F32) * scale, b_kn_g[l].astype(F32)]).reshape(4, 1, HEAD_DIM)
        qk = _proj_heads(h, w_in, l, (0, 1, 3, 4), seg_cols=D, gains=gains)
        vv = _proj_heads(h, w_in, l, (2, 5), seg_cols=D)
        gates = _proj_heads(h, w_gate, l, (0, 1), seg_cols=D,
                            bias=b_gate[l].reshape(1, -1).astype(F32))
        out_c = _proj_conv(h, w_in, w_gate, l, b_gate[l], c_conv_w[l], seq=S,
                           bg_col=6 * D, cg_col=7 * D, xc_col=8 * D, gc_col=2 * D)

        lambda_init = 0.8 - 0.6 * math.exp(-0.3 * l)
        lam = (jnp.exp(jnp.sum(b_lq1[l].astype(F32) * b_lk1[l].astype(F32)))
               - jnp.exp(jnp.sum(b_lq2[l].astype(F32) * b_lk2[l].astype(F32))) + lambda_init).reshape(1)
        merged = _attention(qk, vv, gates, out_c, _attn_a_bias_tile(a_rel_bias[l]), lam, b_subln_g[l],
                            batch=B, seq=S, b_heads=b_heads, out_scale=1.0 - lambda_init)

        x1, h2 = _merge_out(xf, merged, w_o_b, l, norm2_g[l])
        xf, h = _mlp(x1, h2, w1_b, w2_b, l, norm1_g[l + 1] if l + 1 < depth else None)
    return xf.reshape(B, S, D).astype(x.dtype)
```

```python
import functools
import math

import jax
import jax.numpy as jnp
from jax import lax
from jax.experimental import pallas as pl
from jax.experimental.pallas import tpu as pltpu

F32 = jnp.float32
BF16 = jnp.bfloat16

LANE = 128
F32_SUBLANES = 8
CHUNK = 64
HEAD_DIM = 128
EPS = 1e-6
A_LEFT_CHUNKS = 8
A_MAX_REL = 128
A_QBLOCK = 2 * CHUNK
A_BAND = (A_LEFT_CHUNKS + 2) * CHUNK
B_QBLOCK = 256
ATTN_B_HEADS = 1
PROJ_ROW_CHUNK = 256
CONV_COL_TILE = 256
MASKED = -1e30
LOG2E = math.log2(math.e)
VMEM_LIMIT = 56 * 1024 * 1024


def _params(*sem):
    return pltpu.CompilerParams(dimension_semantics=sem, vmem_limit_bytes=VMEM_LIMIT)


def _tile(n, pref):
    if n <= pref:
        return n
    t = pref
    while n % t:
        t -= LANE
    assert t > 0
    return t


def _rmsnorm_kernel(x_ref, g_ref, o_ref):
    x = x_ref[...]
    ms = jnp.mean(x * x, axis=-1, keepdims=True)
    o_ref[...] = (x * lax.rsqrt(ms + EPS) * g_ref[...]).astype(o_ref.dtype)


def _rmsnorm(x, g):
    T, D = x.shape
    tm = _tile(T, 512)
    return pl.pallas_call(
        _rmsnorm_kernel,
        grid=(T // tm,),
        in_specs=[pl.BlockSpec((tm, D), lambda i: (i, 0)),
                  pl.BlockSpec((1, D), lambda i: (0, 0))],
        out_specs=pl.BlockSpec((tm, D), lambda i: (i, 0)),
        out_shape=jax.ShapeDtypeStruct((T, D), BF16),
        compiler_params=_params("parallel"),
        name="rmsnorm",
    )(x, g.reshape(1, D).astype(F32))


def _cast_weights(w_refs, wb_ref):
    @pl.when(pl.program_id(1) == 0)
    def _():
        for n, w_ref in enumerate(w_refs):
            wb_ref[n] = w_ref[...].astype(wb_ref.dtype)


def _proj_heads_kernel(h_ref, w_ref, *rest, epilogue):
    p_ref, (o_ref, wb_ref) = rest[0], rest[-2:]
    _cast_weights([w_ref], wb_ref)
    tm = h_ref.shape[0]
    rc = min(tm, PROJ_ROW_CHUNK)
    for r in range(tm // rc):
        rows = slice(r * rc, (r + 1) * rc)
        acc = jnp.dot(h_ref[rows, :], wb_ref[0], preferred_element_type=F32)
        for g in range(o_ref.shape[0]):
            cols = slice(g * LANE, (g + 1) * LANE)
            a = acc[:, cols]
            if epilogue == "norm":
                ms = jnp.mean(a * a, axis=-1, keepdims=True)
                a = a * lax.rsqrt(ms + EPS) * p_ref[...]
            elif epilogue == "gate":
                a = jax.nn.sigmoid(a + p_ref[:, cols])
            o_ref[g, rows, :] = a.astype(o_ref.dtype)


def _proj_heads(h, w, layer, segs, *, seg_cols, gains=None, bias=None):
    T, D = h.shape
    tm = _tile(T, 2048)
    tn = _tile(seg_cols, 1024)
    tps = seg_cols // tn
    n_cols = len(segs) * seg_cols

    def col_block(j):
        k = j // tps
        seg = sum(jnp.where(k == n, s, 0) for n, s in enumerate(segs))
        return seg * tps + j % tps

    if gains is not None:
        epilogue, p, p_spec = "norm", gains, pl.BlockSpec((None, 1, LANE), lambda j, i: (j // tps, 0, 0))
    elif bias is not None:
        epilogue, p, p_spec = "gate", bias, pl.BlockSpec((1, tn), lambda j, i: (0, col_block(j)))
    else:
        epilogue, p, p_spec = "plain", None, None
    return pl.pallas_call(
        functools.partial(_proj_heads_kernel, epilogue=epilogue),
        grid=(n_cols // tn, T // tm),
        in_specs=[pl.BlockSpec((tm, D), lambda j, i: (i, 0)),
                  pl.BlockSpec((None, D, tn), lambda j, i: (layer, 0, col_block(j)))]
                 + ([] if p is None else [p_spec]),
        out_specs=pl.BlockSpec((tn // LANE, tm, LANE), lambda j, i: (j, i, 0)),
        out_shape=jax.ShapeDtypeStruct((n_cols // LANE, T, LANE), BF16),
        scratch_shapes=[pltpu.VMEM((1, D, tn), BF16)],
        compiler_params=_params("arbitrary", "arbitrary"),
        name="proj_heads_" + epilogue,
    )(h, w, *([] if p is None else [p]))


def _proj_conv_kernel(h_ref, wbg_ref, wcg_ref, wxc_ref, wgc_ref, b_ref, cw_ref, *rest, n_cast):
    o_ref, wb_ref = rest[n_cast], rest[-1]
    for src_ref, dst_ref in zip(rest[:n_cast], rest[n_cast + 1:-1]):
        dst_ref[...] = src_ref[...].astype(dst_ref.dtype)
    _cast_weights([wbg_ref, wcg_ref, wxc_ref, wgc_ref], wb_ref)
    S = h_ref.shape[0]
    rc = min(S, PROJ_ROW_CHUNK)
    row = lax.broadcasted_iota(jnp.int32, (rc, 1), 0)
    tail = jnp.zeros((F32_SUBLANES, o_ref.shape[1]), F32)
    for r0 in range(0, S, rc):
        rows = slice(r0, r0 + rc)
        hc = h_ref[rows, :]
        bg, cg, xc, gc = [jnp.dot(hc, wb_ref[n], preferred_element_type=F32) for n in range(4)]
        u = cg * xc
        prev1, prev2 = tail[F32_SUBLANES - 1:, :], tail[F32_SUBLANES - 2:F32_SUBLANES - 1, :]
        u1 = jnp.where(row == 0, prev1, pltpu.roll(u, 1, 0))
        u2 = jnp.where(row == 0, prev2, jnp.where(row == 1, prev1, pltpu.roll(u, 2, 0)))
        conv = cw_ref[0:1, :] * u2 + cw_ref[1:2, :] * u1 + cw_ref[2:3, :] * u
        o_ref[rows, :] = (jax.nn.sigmoid(gc + b_ref[...]) * (bg * conv)).astype(o_ref.dtype)
        tail = u[rc - F32_SUBLANES:, :]


def _proj_conv(h, w_in, w_gate, layer, b_gate, conv_w, to_cast=(), *, seq, bg_col, cg_col, xc_col, gc_col):
    T, D = h.shape
    tn = _tile(D, CONV_COL_TILE)
    n_seq = T // seq
    steps = (D // tn) * n_seq

    def w_spec(col0):
        return pl.BlockSpec((None, D, tn), lambda j, i: (layer, 0, col0 // tn + j))

    slab_in, slab_out, slab_shape = [], [], []
    for w in to_cast:
        _, R, C = w.shape
        assert R % (steps * 2 * F32_SUBLANES) == 0, "row slabs must hold whole packed bf16 register tiles"
        slab_in.append(pl.BlockSpec((None, R // steps, C), lambda j, i: (layer, j * n_seq + i, 0)))
        slab_out.append(pl.BlockSpec((R // steps, C), lambda j, i: (j * n_seq + i, 0)))
        slab_shape.append(jax.ShapeDtypeStruct((R, C), BF16))
    out = pl.pallas_call(
        functools.partial(_proj_conv_kernel, n_cast=len(to_cast)),
        grid=(D // tn, n_seq),
        in_specs=[pl.BlockSpec((seq, D), lambda j, i: (i, 0)),
                  w_spec(bg_col), w_spec(cg_col), w_spec(xc_col), w_spec(gc_col),
                  pl.BlockSpec((1, tn), lambda j, i: (0, gc_col // tn + j)),
                  pl.BlockSpec((3, tn), lambda j, i: (0, j))] + slab_in,
        out_specs=[pl.BlockSpec((seq, tn), lambda j, i: (i, j))] + slab_out,
        out_shape=[jax.ShapeDtypeStruct((T, D), BF16)] + slab_shape,
        scratch_shapes=[pltpu.VMEM((4, D, tn), BF16)],
        compiler_params=_params("arbitrary", "arbitrary"),
        name="proj_conv",
    )(h, w_in, w_in, w_in, w_gate, b_gate.reshape(1, -1).astype(F32), conv_w.astype(F32), *to_cast)
    return out[0], out[1:]


def _attn_a_bias_tile(rel_bias):
    H = rel_bias.shape[0]
    table = rel_bias.astype(F32)
    n_far = A_LEFT_CHUNKS * CHUNK - A_MAX_REL + CHUNK - 1
    far = jnp.broadcast_to(table[:, -1:], (H, n_far))
    near = table[:, :A_MAX_REL - CHUNK:-1]
    f = jnp.concatenate([far, near, jnp.zeros((H, 1), F32)], axis=1)
    L = f.shape[1] - 1
    windows = jnp.tile(f, (1, CHUNK))[:, :CHUNK * L].reshape(H, CHUNK, L)
    t = windows[:, :, CHUNK - 1:]
    masked = jnp.full((H, CHUNK, CHUNK), MASKED, F32)
    t = t * LOG2E
    return jnp.concatenate([jnp.concatenate([t, masked], axis=2),
                            jnp.concatenate([masked, t], axis=2)], axis=1)


def _attn_a_steps(q_ref, k_ref, v_ref, bias_ref, gate_ref, o_ref, vx_ref):
    S = q_ref.shape[0]
    n_blocks = S // A_QBLOCK
    vx_ref[:, :HEAD_DIM] = v_ref[...]
    vx_ref[:, HEAD_DIM:] = jnp.ones((S, HEAD_DIM), vx_ref.dtype)

    def band(blk):
        k1 = (blk + 1) * A_QBLOCK
        return max(0, k1 - A_BAND), k1

    def scores(blk):
        k0, k1 = band(blk)
        b0 = A_BAND - (k1 - k0)
        s = lax.dot_general(q_ref[blk * A_QBLOCK:k1, :], k_ref[k0:k1, :], (((1,), (1,)), ((), ())),
                            preferred_element_type=F32)
        return s + bias_ref[:, b0:]

    s_next = scores(0)
    for blk in range(n_blocks):
        s = s_next
        if blk + 1 < n_blocks:
            s_next = scores(blk + 1)
        k0, k1 = band(blk)
        e = jnp.exp2(s - jnp.max(s, axis=-1, keepdims=True))
        ox = jnp.dot(e.astype(BF16), vx_ref[k0:k1, :], preferred_element_type=F32)
        rows = slice(blk * A_QBLOCK, k1)
        o_ref[rows, :] = (ox[:, :HEAD_DIM] / ox[:, HEAD_DIM:]).astype(o_ref.dtype) * gate_ref[rows, :]
        yield


def _attn_b_steps(lam_ref, q_ref, k_ref, v_ref, g_ref, gate_ref, o_ref, *, out_scale):
    S = q_ref.shape[1]
    lam = lam_ref[0]
    v = jnp.concatenate([v_ref[0], v_ref[1]], axis=-1)
    chunk = lax.broadcasted_iota(jnp.int32, (B_QBLOCK, HEAD_DIM), 0) // CHUNK
    lane = lax.broadcasted_iota(jnp.int32, (B_QBLOCK, HEAD_DIM), 1)
    q_hot = jnp.where(lane == chunk, 1.0, 0.0).astype(q_ref.dtype)
    k_hot = jnp.where((lane < B_QBLOCK // CHUNK) & (chunk > lane), MASKED, 0.0).astype(k_ref.dtype)
    gs = g_ref[...] * out_scale
    nt = (((1,), (1,)), ((), ()))

    def scores(blk):
        q0, q1 = blk * B_QBLOCK, (blk + 1) * B_QBLOCK
        out = []
        for m in range(2):
            q = q_ref[m, q0:q1, :]
            sd = lax.dot_general(jnp.concatenate([q, q_hot], axis=1),
                                 jnp.concatenate([k_ref[m, q0:q1, :], k_hot], axis=1), nt,
                                 preferred_element_type=F32)
            sl = lax.dot_general(q, k_ref[m, 0:q0, :], nt, preferred_element_type=F32) if blk else None
            out.append((sl, sd))
        return out

    def softmax_parts(sl, sd):
        mx = jnp.max(sd, axis=-1, keepdims=True)
        el = None
        if sl is not None:
            mx = jnp.maximum(mx, jnp.max(sl, axis=-1, keepdims=True))
            el = jnp.exp2(sl - mx)
        ed = jnp.exp2(sd - mx)
        l = jnp.sum(ed, axis=-1, keepdims=True)
        if sl is not None:
            l = l + jnp.sum(el, axis=-1, keepdims=True)
        return el, ed, l

    n_blocks = S // B_QBLOCK
    s_next = scores(0)
    for blk in range(n_blocks):
        q0, q1 = blk * B_QBLOCK, (blk + 1) * B_QBLOCK
        (sl1, sd1), (sl2, sd2) = s_next
        if blk + 1 < n_blocks:
            s_next = scores(blk + 1)
        yield
        el1, ed1, l1 = softmax_parts(sl1, sd1)
        el2, ed2, l2 = softmax_parts(sl2, sd2)
        r = lam * l1 / l2
        ob = jnp.dot((ed1 - ed2 * r).astype(BF16), v[q0:q1, :], preferred_element_type=F32)
        if blk:
            ob = ob + jnp.dot((el1 - el2 * r).astype(BF16), v[0:q0, :], preferred_element_type=F32)
        inv_l1 = 1.0 / l1
        ms = jnp.mean(ob * ob, axis=-1, keepdims=True) * (inv_l1 * inv_l1)
        y = (ob * (inv_l1 * lax.rsqrt(ms + EPS)) * gs).astype(o_ref.dtype)
        for half in range(2):
            cols = slice(half * HEAD_DIM, (half + 1) * HEAD_DIM)
            o_ref[q0:q1, cols] = y[:, cols] * gate_ref[half, q0:q1, :]
        yield


def _attn_kernel(lam_ref, qa_ref, ka_ref, va_ref, bias_ref, ga_ref, qb_ref, kb_ref, vb_ref, g_ref, gb_ref,
                 oa_ref, ob_ref, vx_ref, *, out_scale):
    S = qa_ref.shape[1]
    for i in range(ob_ref.shape[0]):
        pair = pl.ds(2 * i, 2)
        for _ in _attn_b_steps(lam_ref, qb_ref.at[pair], kb_ref.at[pair], vb_ref.at[pair], g_ref, gb_ref.at[pair],
                               ob_ref.at[i], out_scale=out_scale):
            pass
    a_progs = [_attn_a_steps(qa_ref.at[i], ka_ref.at[i], va_ref.at[i], bias_ref.at[i], ga_ref.at[i],
                             oa_ref.at[i], vx_ref.at[i]) for i in range(qa_ref.shape[0])]
    for _ in range(S // A_QBLOCK):
        for prog in a_progs:
            next(prog)


def _attention(qk, vv, gates, bias_tile, lam, subln_g, *, batch, seq, b_heads, out_scale):
    T = qk.shape[1]
    nb = math.gcd(ATTN_B_HEADS, b_heads)
    steps = b_heads // nb

    def groups(seg):
        return pl.BlockSpec((2 * nb, seq, HEAD_DIM), lambda b, h: (seg * steps + h, b, 0))

    return pl.pallas_call(
        functools.partial(_attn_kernel, out_scale=out_scale),
        grid=(batch, steps),
        in_specs=[pl.BlockSpec(memory_space=pltpu.SMEM),
                  groups(0), groups(1), groups(0),
                  pl.BlockSpec((2 * nb, A_QBLOCK, A_BAND), lambda b, h: (h, 0, 0)), groups(0),
                  groups(2), groups(3), groups(1),
                  pl.BlockSpec((1, 2 * HEAD_DIM), lambda b, h: (0, 0)), groups(1)],
        out_specs=[pl.BlockSpec((2 * nb, seq, HEAD_DIM), lambda b, h: (h, b, 0)),
                   pl.BlockSpec((nb, seq, 2 * HEAD_DIM), lambda b, h: (h, b, 0))],
        out_shape=[jax.ShapeDtypeStruct((2 * b_heads, T, HEAD_DIM), BF16),
                   jax.ShapeDtypeStruct((b_heads, T, 2 * HEAD_DIM), BF16)],
        scratch_shapes=[pltpu.VMEM((2 * nb, seq, 2 * HEAD_DIM), BF16)],
        compiler_params=_params("parallel", "parallel"),
        name="attention",
    )(lam, qk, qk, vv, bias_tile, gates, qk, qk, vv, subln_g.reshape(1, 2 * HEAD_DIM).astype(F32), gates)


def _merge_out_kernel(x_ref, oa_ref, ob_ref, oc_ref, wo_ref, g2_ref, x1_ref, h2_ref, m_ref):
    D = x_ref.shape[1]
    for g in range(D // LANE):
        sl = slice(g * LANE, (g + 1) * LANE)
        half = slice((g % 2) * LANE, (g % 2 + 1) * LANE)
        m_ref[:, sl] = oa_ref[g] + ob_ref[g // 2, :, half] + oc_ref[:, sl]
    x1 = x_ref[...] + jnp.dot(m_ref[...], wo_ref[...], preferred_element_type=F32)
    x1_ref[...] = x1
    ms = jnp.mean(x1 * x1, axis=-1, keepdims=True)
    h2_ref[...] = (x1 * lax.rsqrt(ms + EPS) * g2_ref[...]).astype(h2_ref.dtype)


def _merge_out(x, out_a, out_b, out_c, w_o, layer, g2):
    T, D = x.shape
    tm = _tile(T, 512)
    row = lambda i: (i, 0)
    once = pl.Buffered(1)
    return pl.pallas_call(
        _merge_out_kernel,
        grid=(T // tm,),
        in_specs=[pl.BlockSpec((tm, D), row),
                  pl.BlockSpec((out_a.shape[0], tm, HEAD_DIM), lambda i: (0, i, 0)),
                  pl.BlockSpec((out_b.shape[0], tm, 2 * HEAD_DIM), lambda i: (0, i, 0)),
                  pl.BlockSpec((tm, D), row),
                  pl.BlockSpec((None, D, D), lambda i: (layer, 0, 0), pipeline_mode=once),
                  pl.BlockSpec((1, D), lambda i: (0, 0), pipeline_mode=once)],
        out_specs=[pl.BlockSpec((tm, D), row), pl.BlockSpec((tm, D), row)],
        out_shape=[jax.ShapeDtypeStruct((T, D), F32), jax.ShapeDtypeStruct((T, D), BF16)],
        scratch_shapes=[pltpu.VMEM((tm, D), BF16)],
        compiler_params=_params("parallel"),
        name="merge_out",
    )(x, out_a, out_b, out_c, w_o, g2.reshape(1, D).astype(F32))


def _mlp_kernel(x1_ref, h2_ref, w1_ref, w2_ref, *rest, with_next):
    x2_ref = rest[1] if with_next else rest[0]
    f = pl.program_id(1)
    tm = h2_ref.shape[0]
    rc = min(tm, PROJ_ROW_CHUNK)
    chunks = [slice(r * rc, (r + 1) * rc) for r in range(tm // rc)]

    @pl.when(f == 0)
    def _():
        x2_ref[...] = x1_ref[...]

    acts = [jnp.dot(h2_ref[rows, :], w1_ref[...], preferred_element_type=F32) for rows in chunks]
    for rows, a in zip(chunks, acts):
        a = jnp.square(jnp.maximum(a, 0.0)).astype(BF16)
        x2_ref[rows, :] += jnp.dot(a, w2_ref[...], preferred_element_type=F32)

    if with_next:
        @pl.when(f == pl.num_programs(1) - 1)
        def _():
            x2 = x2_ref[...]
            ms = jnp.mean(x2 * x2, axis=-1, keepdims=True)
            rest[2][...] = (x2 * lax.rsqrt(ms + EPS) * rest[0][...]).astype(BF16)


def _mlp(x1, h2, w1, w2, g_next):
    T, D = x1.shape
    FF = w1.shape[1]
    tm = _tile(T, 512)
    tf = _tile(FF, 1024)
    with_next = g_next is not None
    row = lambda i, f: (i, 0)
    in_specs = [pl.BlockSpec((tm, D), row),
                pl.BlockSpec((tm, D), row),
                pl.BlockSpec((D, tf), lambda i, f: (0, f)),
                pl.BlockSpec((tf, D), lambda i, f: (f, 0))]
    args = [x1, h2, w1, w2]
    out_specs = [pl.BlockSpec((tm, D), row)]
    out_shape = [jax.ShapeDtypeStruct((T, D), F32)]
    if with_next:
        in_specs.append(pl.BlockSpec((1, D), lambda i, f: (0, 0)))
        args.append(g_next.reshape(1, D).astype(F32))
        out_specs.append(pl.BlockSpec((tm, D), row))
        out_shape.append(jax.ShapeDtypeStruct((T, D), BF16))
    out = pl.pallas_call(
        functools.partial(_mlp_kernel, with_next=with_next),
        grid=(T // tm, FF // tf),
        in_specs=in_specs,
        out_specs=out_specs,
        out_shape=out_shape,
        compiler_params=_params("parallel", "arbitrary"),
        name="mlp",
    )(*args)
    return (out[0], out[1]) if with_next else (out[0], None)


def kernel(x, norm1_g, w_in, w_gate, b_gate, a_qn_g, a_kn_g, a_rel_bias, b_qn_g, b_kn_g,
           b_lq1, b_lk1, b_lq2, b_lk2, b_subln_g, c_conv_w, w_o, norm2_g, w_mlp1, w_mlp2):
    B, S, D = x.shape
    depth = w_in.shape[0]
    T = B * S
    b_heads = D // (2 * HEAD_DIM)
    assert S % B_QBLOCK == 0 and D % (2 * HEAD_DIM) == 0 and w_in.shape[2] == 9 * D
    scale = HEAD_DIM ** -0.5 * LOG2E

    xf = x.reshape(T, D).astype(F32)
    w_in, w_gate = w_in.astype(F32), w_gate.astype(F32)
    w_o_b, w_mlp1, w_mlp2 = w_o.astype(BF16), w_mlp1.astype(F32), w_mlp2.astype(F32)

    h = _rmsnorm(xf, norm1_g[0])
    for l in range(depth):
        gains = jnp.stack([a_qn_g[l].astype(F32) * scale, a_kn_g[l].astype(F32),
                           b_qn_g[l].astype(F32) * scale, b_kn_g[l].astype(F32)]).reshape(4, 1, HEAD_DIM)
        qk = _proj_heads(h, w_in, l, (0, 1, 3, 4), seg_cols=D, gains=gains)
        vv = _proj_heads(h, w_in, l, (2, 5), seg_cols=D)
        gates = _proj_heads(h, w_gate, l, (0, 1), seg_cols=D,
                            bias=b_gate[l].reshape(1, -1).astype(F32))
        out_c, (w1_b, w2_b) = _proj_conv(h, w_in, w_gate, l, b_gate[l], c_conv_w[l], (w_mlp1, w_mlp2), seq=S,
                                         bg_col=6 * D, cg_col=7 * D, xc_col=8 * D, gc_col=2 * D)

        lambda_init = 0.8 - 0.6 * math.exp(-0.3 * l)
        lam = (jnp.exp(jnp.sum(b_lq1[l].astype(F32) * b_lk1[l].astype(F32)))
               - jnp.exp(jnp.sum(b_lq2[l].astype(F32) * b_lk2[l].astype(F32))) + lambda_init).reshape(1)
        out_a, out_b = _attention(qk, vv, gates, _attn_a_bias_tile(a_rel_bias[l]), lam, b_subln_g[l],
                                  batch=B, seq=S, b_heads=b_heads, out_scale=1.0 - lambda_init)

        x1, h2 = _merge_out(xf, out_a, out_b, out_c, w_o_b, l, norm2_g[l])
        xf, h = _mlp(x1, h2, w1_b, w2_b, norm1_g[l + 1] if l + 1 < depth else None)
    return xf.reshape(B, S, D).astype(x.dtype)
```

```python
import functools
import math

import jax
import jax.numpy as jnp
from jax import lax
from jax.experimental import pallas as pl
from jax.experimental.pallas import tpu as pltpu

F32 = jnp.float32
BF16 = jnp.bfloat16

LANE = 128
F32_SUBLANES = 8
CHUNK = 64
HEAD_DIM = 128
EPS = 1e-6
A_LEFT_CHUNKS = 8
A_MAX_REL = 128
A_QBLOCK = 2 * CHUNK
A_BAND = (A_LEFT_CHUNKS + 2) * CHUNK
B_QBLOCK = 256
ATTN_B_HEADS = 1
PROJ_ROW_CHUNK = 256
CONV_COL_TILE = 256
MASKED = -1e30
LOG2E = math.log2(math.e)
VMEM_LIMIT = 56 * 1024 * 1024


def _params(*sem):
    return pltpu.CompilerParams(dimension_semantics=sem, vmem_limit_bytes=VMEM_LIMIT)


def _tile(n, pref):
    if n <= pref:
        return n
    t = pref
    while n % t:
        t -= LANE
    assert t > 0
    return t


def _rmsnorm_kernel(x_ref, g_ref, o_ref):
    x = x_ref[...]
    ms = jnp.mean(x * x, axis=-1, keepdims=True)
    o_ref[...] = (x * lax.rsqrt(ms + EPS) * g_ref[...]).astype(o_ref.dtype)


def _rmsnorm(x, g):
    T, D = x.shape
    tm = _tile(T, 512)
    return pl.pallas_call(
        _rmsnorm_kernel,
        grid=(T // tm,),
        in_specs=[pl.BlockSpec((tm, D), lambda i: (i, 0)),
                  pl.BlockSpec((1, D), lambda i: (0, 0))],
        out_specs=pl.BlockSpec((tm, D), lambda i: (i, 0)),
        out_shape=jax.ShapeDtypeStruct((T, D), BF16),
        compiler_params=_params("parallel"),
        name="rmsnorm",
    )(x, g.reshape(1, D).astype(F32))


def _cast_weights(w_refs, wb_ref):
    @pl.when(pl.program_id(1) == 0)
    def _():
        for n, w_ref in enumerate(w_refs):
            wb_ref[n] = w_ref[...].astype(wb_ref.dtype)


def _proj_heads_kernel(h_ref, w_ref, *rest, epilogue):
    p_ref, (o_ref, wb_ref) = rest[0], rest[-2:]
    _cast_weights([w_ref], wb_ref)
    tm = h_ref.shape[0]
    rc = min(tm, PROJ_ROW_CHUNK)
    for r in range(tm // rc):
        rows = slice(r * rc, (r + 1) * rc)
        acc = jnp.dot(h_ref[rows, :], wb_ref[0], preferred_element_type=F32)
        for g in range(o_ref.shape[0]):
            cols = slice(g * LANE, (g + 1) * LANE)
            a = acc[:, cols]
            if epilogue == "norm":
                ms = jnp.mean(a * a, axis=-1, keepdims=True)
                a = a * lax.rsqrt(ms + EPS) * p_ref[...]
            elif epilogue == "gate":
                a = jax.nn.sigmoid(a + p_ref[:, cols])
            o_ref[g, rows, :] = a.astype(o_ref.dtype)


def _proj_heads(h, w, layer, segs, *, seg_cols, gains=None, bias=None):
    T, D = h.shape
    tm = _tile(T, 2048)
    tn = _tile(seg_cols, 1024)
    tps = seg_cols // tn
    n_cols = len(segs) * seg_cols

    def col_block(j):
        k = j // tps
        seg = sum(jnp.where(k == n, s, 0) for n, s in enumerate(segs))
        return seg * tps + j % tps

    if gains is not None:
        epilogue, p, p_spec = "norm", gains, pl.BlockSpec((None, 1, LANE), lambda j, i: (j // tps, 0, 0))
    elif bias is not None:
        epilogue, p, p_spec = "gate", bias, pl.BlockSpec((1, tn), lambda j, i: (0, col_block(j)))
    else:
        epilogue, p, p_spec = "plain", None, None
    return pl.pallas_call(
        functools.partial(_proj_heads_kernel, epilogue=epilogue),
        grid=(n_cols // tn, T // tm),
        in_specs=[pl.BlockSpec((tm, D), lambda j, i: (i, 0)),
                  pl.BlockSpec((None, D, tn), lambda j, i: (layer, 0, col_block(j)))]
                 + ([] if p is None else [p_spec]),
        out_specs=pl.BlockSpec((tn // LANE, tm, LANE), lambda j, i: (j, i, 0)),
        out_shape=jax.ShapeDtypeStruct((n_cols // LANE, T, LANE), BF16),
        scratch_shapes=[pltpu.VMEM((1, D, tn), BF16)],
        compiler_params=_params("arbitrary", "arbitrary"),
        name="proj_heads_" + epilogue,
    )(h, w, *([] if p is None else [p]))


def _proj_conv_kernel(h_ref, wbg_ref, wcg_ref, wxc_ref, wgc_ref, b_ref, cw_ref, *rest, n_cast):
    o_ref, wb_ref = rest[n_cast], rest[-1]
    for src_ref, dst_ref in zip(rest[:n_cast], rest[n_cast + 1:-1]):
        dst_ref[...] = src_ref[...].astype(dst_ref.dtype)
    _cast_weights([wbg_ref, wcg_ref, wxc_ref, wgc_ref], wb_ref)
    S = h_ref.shape[0]
    rc = min(S, PROJ_ROW_CHUNK)
    row = lax.broadcasted_iota(jnp.int32, (rc, 1), 0)
    tail = jnp.zeros((F32_SUBLANES, o_ref.shape[1]), F32)
    for r0 in range(0, S, rc):
        rows = slice(r0, r0 + rc)
        hc = h_ref[rows, :]
        bg, cg, xc, gc = [jnp.dot(hc, wb_ref[n], preferred_element_type=F32) for n in range(4)]
        u = cg * xc
        prev1, prev2 = tail[F32_SUBLANES - 1:, :], tail[F32_SUBLANES - 2:F32_SUBLANES - 1, :]
        u1 = jnp.where(row == 0, prev1, pltpu.roll(u, 1, 0))
        u2 = jnp.where(row == 0, prev2, jnp.where(row == 1, prev1, pltpu.roll(u, 2, 0)))
        conv = cw_ref[0:1, :] * u2 + cw_ref[1:2, :] * u1 + cw_ref[2:3, :] * u
        o_ref[rows, :] = (jax.nn.sigmoid(gc + b_ref[...]) * (bg * conv)).astype(o_ref.dtype)
        tail = u[rc - F32_SUBLANES:, :]


def _proj_conv(h, w_in, w_gate, layer, b_gate, conv_w, to_cast=(), *, seq, bg_col, cg_col, xc_col, gc_col):
    T, D = h.shape
    tn = _tile(D, CONV_COL_TILE)
    n_seq = T // seq
    steps = (D // tn) * n_seq

    def w_spec(col0):
        return pl.BlockSpec((None, D, tn), lambda j, i: (layer, 0, col0 // tn + j))

    slab_in, slab_out, slab_shape = [], [], []
    for w in to_cast:
        _, R, C = w.shape
        assert R % (steps * 2 * F32_SUBLANES) == 0, "row slabs must hold whole packed bf16 register tiles"
        slab_in.append(pl.BlockSpec((None, R // steps, C), lambda j, i: (layer, j * n_seq + i, 0)))
        slab_out.append(pl.BlockSpec((R // steps, C), lambda j, i: (j * n_seq + i, 0)))
        slab_shape.append(jax.ShapeDtypeStruct((R, C), BF16))
    out = pl.pallas_call(
        functools.partial(_proj_conv_kernel, n_cast=len(to_cast)),
        grid=(D // tn, n_seq),
        in_specs=[pl.BlockSpec((seq, D), lambda j, i: (i, 0)),
                  w_spec(bg_col), w_spec(cg_col), w_spec(xc_col), w_spec(gc_col),
                  pl.BlockSpec((1, tn), lambda j, i: (0, gc_col // tn + j)),
                  pl.BlockSpec((3, tn), lambda j, i: (0, j))] + slab_in,
        out_specs=[pl.BlockSpec((seq, tn), lambda j, i: (i, j))] + slab_out,
        out_shape=[jax.ShapeDtypeStruct((T, D), BF16)] + slab_shape,
        scratch_shapes=[pltpu.VMEM((4, D, tn), BF16)],
        compiler_params=_params("arbitrary", "arbitrary"),
        name="proj_conv",
    )(h, w_in, w_in, w_in, w_gate, b_gate.reshape(1, -1).astype(F32), conv_w.astype(F32), *to_cast)
    return out[0], out[1:]


def _attn_a_bias_tile(rel_bias):
    H = rel_bias.shape[0]
    table = rel_bias.astype(F32)
    n_far = A_LEFT_CHUNKS * CHUNK - A_MAX_REL + CHUNK - 1
    far = jnp.broadcast_to(table[:, -1:], (H, n_far))
    near = table[:, :A_MAX_REL - CHUNK:-1]
    f = jnp.concatenate([far, near, jnp.zeros((H, 1), F32)], axis=1)
    L = f.shape[1] - 1
    windows = jnp.tile(f, (1, CHUNK))[:, :CHUNK * L].reshape(H, CHUNK, L)
    t = windows[:, :, CHUNK - 1:]
    masked = jnp.full((H, CHUNK, CHUNK), MASKED, F32)
    t = t * LOG2E
    return jnp.concatenate([jnp.concatenate([t, masked], axis=2),
                            jnp.concatenate([masked, t], axis=2)], axis=1)


def _attn_a_steps(q_ref, k_ref, v_ref, bias_ref, gate_ref, o_ref, vx_ref):
    S = q_ref.shape[0]
    n_blocks = S // A_QBLOCK
    vx_ref[:, :HEAD_DIM] = v_ref[...]
    vx_ref[:, HEAD_DIM:] = jnp.ones((S, HEAD_DIM), vx_ref.dtype)

    def band(blk):
        k1 = (blk + 1) * A_QBLOCK
        return max(0, k1 - A_BAND), k1

    def scores(blk):
        k0, k1 = band(blk)
        b0 = A_BAND - (k1 - k0)
        s = lax.dot_general(q_ref[blk * A_QBLOCK:k1, :], k_ref[k0:k1, :], (((1,), (1,)), ((), ())),
                            preferred_element_type=F32)
        return s + bias_ref[:, b0:]

    s_next = scores(0)
    for blk in range(n_blocks):
        s = s_next
        if blk + 1 < n_blocks:
            s_next = scores(blk + 1)
        k0, k1 = band(blk)
        e = jnp.exp2(s - jnp.max(s, axis=-1, keepdims=True))
        ox = jnp.dot(e.astype(BF16), vx_ref[k0:k1, :], preferred_element_type=F32)
        rows = slice(blk * A_QBLOCK, k1)
        o_ref[rows, :] = (ox[:, :HEAD_DIM] / ox[:, HEAD_DIM:]).astype(o_ref.dtype) * gate_ref[rows, :]
        yield


def _attn_b_steps(lam_ref, q_ref, k_ref, v_ref, g_ref, gate_ref, o_ref, *, out_scale):
    S = q_ref.shape[1]
    lam = lam_ref[0]
    v = jnp.concatenate([v_ref[0], v_ref[1]], axis=-1)
    chunk = lax.broadcasted_iota(jnp.int32, (B_QBLOCK, HEAD_DIM), 0) // CHUNK
    lane = lax.broadcasted_iota(jnp.int32, (B_QBLOCK, HEAD_DIM), 1)
    q_hot = jnp.where(lane == chunk, 1.0, 0.0).astype(q_ref.dtype)
    k_hot = jnp.where((lane < B_QBLOCK // CHUNK) & (chunk > lane), MASKED, 0.0).astype(k_ref.dtype)
    gs = g_ref[...] * out_scale
    nt = (((1,), (1,)), ((), ()))

    def scores(blk):
        q0, q1 = blk * B_QBLOCK, (blk + 1) * B_QBLOCK
        out = []
        for m in range(2):
            q = q_ref[m, q0:q1, :]
            sd = lax.dot_general(jnp.concatenate([q, q_hot], axis=1),
                                 jnp.concatenate([k_ref[m, q0:q1, :], k_hot], axis=1), nt,
                                 preferred_element_type=F32)
            sl = lax.dot_general(q, k_ref[m, 0:q0, :], nt, preferred_element_type=F32) if blk else None
            out.append((sl, sd))
        return out

    def softmax_parts(sl, sd):
        mx = jnp.max(sd, axis=-1, keepdims=True)
        el = None
        if sl is not None:
            mx = jnp.maximum(mx, jnp.max(sl, axis=-1, keepdims=True))
            el = jnp.exp2(sl - mx)
        ed = jnp.exp2(sd - mx)
        l = jnp.sum(ed, axis=-1, keepdims=True)
        if sl is not None:
            l = l + jnp.sum(el, axis=-1, keepdims=True)
        return el, ed, l

    n_blocks = S // B_QBLOCK
    s_next = scores(0)
    for blk in range(n_blocks):
        q0, q1 = blk * B_QBLOCK, (blk + 1) * B_QBLOCK
        (sl1, sd1), (sl2, sd2) = s_next
        if blk + 1 < n_blocks:
            s_next = scores(blk + 1)
        yield
        el1, ed1, l1 = softmax_parts(sl1, sd1)
        el2, ed2, l2 = softmax_parts(sl2, sd2)
        r = lam * l1 / l2
        ob = jnp.dot((ed1 - ed2 * r).astype(BF16), v[q0:q1, :], preferred_element_type=F32)
        if blk:
            ob = ob + jnp.dot((el1 - el2 * r).astype(BF16), v[0:q0, :], preferred_element_type=F32)
        inv_l1 = 1.0 / l1
        ms = jnp.mean(ob * ob, axis=-1, keepdims=True) * (inv_l1 * inv_l1)
        y = (ob * (inv_l1 * lax.rsqrt(ms + EPS)) * gs).astype(o_ref.dtype)
        for half in range(2):
            cols = slice(half * HEAD_DIM, (half + 1) * HEAD_DIM)
            o_ref[q0:q1, cols] = y[:, cols] * gate_ref[half, q0:q1, :]
        yield


def _attn_kernel(lam_ref, qa_ref, ka_ref, va_ref, bias_ref, ga_ref, qb_ref, kb_ref, vb_ref, g_ref, gb_ref,
                 oa_ref, ob_ref, vx_ref, *, out_scale):
    S = qa_ref.shape[1]
    for i in range(ob_ref.shape[0]):
        pair = pl.ds(2 * i, 2)
        for _ in _attn_b_steps(lam_ref, qb_ref.at[pair], kb_ref.at[pair], vb_ref.at[pair], g_ref, gb_ref.at[pair],
                               ob_ref.at[i], out_scale=out_scale):
            pass
    a_progs = [_attn_a_steps(qa_ref.at[i], ka_ref.at[i], va_ref.at[i], bias_ref.at[i], ga_ref.at[i],
                             oa_ref.at[i], vx_ref.at[i]) for i in range(qa_ref.shape[0])]
    for _ in range(S // A_QBLOCK):
        for prog in a_progs:
            next(prog)


def _attention(qk, vv, gates, bias_tile, lam, subln_g, *, batch, seq, b_heads, out_scale):
    T = qk.shape[1]
    nb = math.gcd(ATTN_B_HEADS, b_heads)
    steps = b_heads // nb

    def groups(seg):
        return pl.BlockSpec((2 * nb, seq, HEAD_DIM), lambda b, h: (seg * steps + h, b, 0))

    return pl.pallas_call(
        functools.partial(_attn_kernel, out_scale=out_scale),
        grid=(batch, steps),
        in_specs=[pl.BlockSpec(memory_space=pltpu.SMEM),
                  groups(0), groups(1), groups(0),
                  pl.BlockSpec((2 * nb, A_QBLOCK, A_BAND), lambda b, h: (h, 0, 0)), groups(0),
                  groups(2), groups(3), groups(1),
                  pl.BlockSpec((1, 2 * HEAD_DIM), lambda b, h: (0, 0)), groups(1)],
        out_specs=[pl.BlockSpec((2 * nb, seq, HEAD_DIM), lambda b, h: (h, b, 0)),
                   pl.BlockSpec((nb, seq, 2 * HEAD_DIM), lambda b, h: (h, b, 0))],
        out_shape=[jax.ShapeDtypeStruct((2 * b_heads, T, HEAD_DIM), BF16),
                   jax.ShapeDtypeStruct((b_heads, T, 2 * HEAD_DIM), BF16)],
        scratch_shapes=[pltpu.VMEM((2 * nb, seq, 2 * HEAD_DIM), BF16)],
        compiler_params=_params("parallel", "parallel"),
        name="attention",
    )(lam, qk, qk, vv, bias_tile, gates, qk, qk, vv, subln_g.reshape(1, 2 * HEAD_DIM).astype(F32), gates)


def _merge_out_kernel(x_ref, oa_ref, ob_ref, oc_ref, wo_ref, g2_ref, x1_ref, h2_ref, m_ref):
    D = x_ref.shape[1]
    for g in range(D // LANE):
        sl = slice(g * LANE, (g + 1) * LANE)
        half = slice((g % 2) * LANE, (g % 2 + 1) * LANE)
        m_ref[:, sl] = oa_ref[g] + ob_ref[g // 2, :, half] + oc_ref[:, sl]
    x1 = x_ref[...] + jnp.dot(m_ref[...], wo_ref[...], preferred_element_type=F32)
    x1_ref[...] = x1
    ms = jnp.mean(x1 * x1, axis=-1, keepdims=True)
    h2_ref[...] = (x1 * lax.rsqrt(ms + EPS) * g2_ref[...]).astype(h2_ref.dtype)


def _merge_out(x, out_a, out_b, out_c, w_o, g2):
    T, D = x.shape
    tm = _tile(T, 512)
    row = lambda i: (i, 0)
    once = pl.Buffered(1)
    return pl.pallas_call(
        _merge_out_kernel,
        grid=(T // tm,),
        in_specs=[pl.BlockSpec((tm, D), row),
                  pl.BlockSpec((out_a.shape[0], tm, HEAD_DIM), lambda i: (0, i, 0)),
                  pl.BlockSpec((out_b.shape[0], tm, 2 * HEAD_DIM), lambda i: (0, i, 0)),
                  pl.BlockSpec((tm, D), row),
                  pl.BlockSpec((D, D), lambda i: (0, 0), pipeline_mode=once),
                  pl.BlockSpec((1, D), lambda i: (0, 0), pipeline_mode=once)],
        out_specs=[pl.BlockSpec((tm, D), row), pl.BlockSpec((tm, D), row)],
        out_shape=[jax.ShapeDtypeStruct((T, D), F32), jax.ShapeDtypeStruct((T, D), BF16)],
        scratch_shapes=[pltpu.VMEM((tm, D), BF16)],
        compiler_params=_params("parallel"),
        name="merge_out",
    )(x, out_a, out_b, out_c, w_o, g2.reshape(1, D).astype(F32))


def _mlp_kernel(x1_ref, h2_ref, w1_ref, w2_ref, *rest, with_next):
    x2_ref = rest[1] if with_next else rest[0]
    f = pl.program_id(1)
    tm = h2_ref.shape[0]
    rc = min(tm, PROJ_ROW_CHUNK)
    chunks = [slice(r * rc, (r + 1) * rc) for r in range(tm // rc)]

    @pl.when(f == 0)
    def _():
        x2_ref[...] = x1_ref[...]

    acts = [jnp.dot(h2_ref[rows, :], w1_ref[...], preferred_element_type=F32) for rows in chunks]
    for rows, a in zip(chunks, acts):
        a = jnp.square(jnp.maximum(a, 0.0)).astype(BF16)
        x2_ref[rows, :] += jnp.dot(a, w2_ref[...], preferred_element_type=F32)

    if with_next:
        @pl.when(f == pl.num_programs(1) - 1)
        def _():
            x2 = x2_ref[...]
            ms = jnp.mean(x2 * x2, axis=-1, keepdims=True)
            rest[2][...] = (x2 * lax.rsqrt(ms + EPS) * rest[0][...]).astype(BF16)


def _mlp(x1, h2, w1, w2, g_next):
    T, D = x1.shape
    FF = w1.shape[1]
    tm = _tile(T, 512)
    tf = _tile(FF, 1024)
    with_next = g_next is not None
    row = lambda i, f: (i, 0)
    in_specs = [pl.BlockSpec((tm, D), row),
                pl.BlockSpec((tm, D), row),
                pl.BlockSpec((D, tf), lambda i, f: (0, f)),
                pl.BlockSpec((tf, D), lambda i, f: (f, 0))]
    args = [x1, h2, w1, w2]
    out_specs = [pl.BlockSpec((tm, D), row)]
    out_shape = [jax.ShapeDtypeStruct((T, D), F32)]
    if with_next:
        in_specs.append(pl.BlockSpec((1, D), lambda i, f: (0, 0)))
        args.append(g_next.reshape(1, D).astype(F32))
        out_specs.append(pl.BlockSpec((tm, D), row))
        out_shape.append(jax.ShapeDtypeStruct((T, D), BF16))
    out = pl.pallas_call(
        functools.partial(_mlp_kernel, with_next=with_next),
        grid=(T // tm, FF // tf),
        in_specs=in_specs,
        out_specs=out_specs,
        out_shape=out_shape,
        compiler_params=_params("parallel", "arbitrary"),
        name="mlp",
    )(*args)
    return (out[0], out[1]) if with_next else (out[0], None)


def kernel(x, norm1_g, w_in, w_gate, b_gate, a_qn_g, a_kn_g, a_rel_bias, b_qn_g, b_kn_g,
           b_lq1, b_lk1, b_lq2, b_lk2, b_subln_g, c_conv_w, w_o, norm2_g, w_mlp1, w_mlp2):
    B, S, D = x.shape
    depth = w_in.shape[0]
    T = B * S
    b_heads = D // (2 * HEAD_DIM)
    assert S % B_QBLOCK == 0 and D % (2 * HEAD_DIM) == 0 and w_in.shape[2] == 9 * D
    scale = HEAD_DIM ** -0.5 * LOG2E

    xf = x.reshape(T, D).astype(F32)
    w_in, w_gate = w_in.astype(F32), w_gate.astype(F32)
    w_o, w_mlp1, w_mlp2 = w_o.astype(F32), w_mlp1.astype(F32), w_mlp2.astype(F32)

    h = _rmsnorm(xf, norm1_g[0])
    for l in range(depth):
        gains = jnp.stack([a_qn_g[l].astype(F32) * scale, a_kn_g[l].astype(F32),
                           b_qn_g[l].astype(F32) * scale, b_kn_g[l].astype(F32)]).reshape(4, 1, HEAD_DIM)
        qk = _proj_heads(h, w_in, l, (0, 1, 3, 4), seg_cols=D, gains=gains)
        vv = _proj_heads(h, w_in, l, (2, 5), seg_cols=D)
        gates = _proj_heads(h, w_gate, l, (0, 1), seg_cols=D,
                            bias=b_gate[l].reshape(1, -1).astype(F32))
        out_c, (w_o_b, w1_b, w2_b) = _proj_conv(h, w_in, w_gate, l, b_gate[l], c_conv_w[l], (w_o, w_mlp1, w_mlp2),
                                                seq=S, bg_col=6 * D, cg_col=7 * D, xc_col=8 * D, gc_col=2 * D)

        lambda_init = 0.8 - 0.6 * math.exp(-0.3 * l)
        lam = (jnp.exp(jnp.sum(b_lq1[l].astype(F32) * b_lk1[l].astype(F32)))
               - jnp.exp(jnp.sum(b_lq2[l].astype(F32) * b_lk2[l].astype(F32))) + lambda_init).reshape(1)
        out_a, out_b = _attention(qk, vv, gates, _attn_a_bias_tile(a_rel_bias[l]), lam, b_subln_g[l],
                                  batch=B, seq=S, b_heads=b_heads, out_scale=1.0 - lambda_init)

        x1, h2 = _merge_out(xf, out_a, out_b, out_c, w_o_b, norm2_g[l])
        xf, h = _mlp(x1, h2, w1_b, w2_b, norm1_g[l + 1] if l + 1 < depth else None)
    return xf.reshape(B, S, D).astype(x.dtype)
```

```python
import functools
import math

import jax
import jax.numpy as jnp
from jax import lax
from jax.experimental import pallas as pl
from jax.experimental.pallas import tpu as pltpu

F32 = jnp.float32
BF16 = jnp.bfloat16

LANE = 128
F32_SUBLANES = 8
CHUNK = 64
HEAD_DIM = 128
EPS = 1e-6
A_LEFT_CHUNKS = 8
A_MAX_REL = 128
A_QBLOCK = 2 * CHUNK
A_BAND = (A_LEFT_CHUNKS + 2) * CHUNK
B_QBLOCK = 256
ATTN_B_HEADS = 1
PROJ_ROW_CHUNK = 256
CONV_COL_TILE = 256
MASKED = -1e30
LOG2E = math.log2(math.e)
VMEM_LIMIT = 56 * 1024 * 1024
MLP_VMEM_LIMIT = 62 * 1024 * 1024
MLP_VMEM_BUDGET = 60 * 1024 * 1024


def _params(*sem):
    return pltpu.CompilerParams(dimension_semantics=sem, vmem_limit_bytes=VMEM_LIMIT)


def _tile(n, pref):
    if n <= pref:
        return n
    t = pref
    while n % t:
        t -= LANE
    assert t > 0
    return t


def _rmsnorm_kernel(x_ref, g_ref, o_ref):
    x = x_ref[...]
    ms = jnp.mean(x * x, axis=-1, keepdims=True)
    o_ref[...] = (x * lax.rsqrt(ms + EPS) * g_ref[...]).astype(o_ref.dtype)


def _rmsnorm(x, g):
    T, D = x.shape
    tm = _tile(T, 512)
    return pl.pallas_call(
        _rmsnorm_kernel,
        grid=(T // tm,),
        in_specs=[pl.BlockSpec((tm, D), lambda i: (i, 0)),
                  pl.BlockSpec((1, D), lambda i: (0, 0))],
        out_specs=pl.BlockSpec((tm, D), lambda i: (i, 0)),
        out_shape=jax.ShapeDtypeStruct((T, D), BF16),
        compiler_params=_params("parallel"),
        name="rmsnorm",
    )(x, g.reshape(1, D).astype(F32))


def _cast_weights(w_refs, wb_ref):
    @pl.when(pl.program_id(1) == 0)
    def _():
        for n, w_ref in enumerate(w_refs):
            wb_ref[n] = w_ref[...].astype(wb_ref.dtype)


def _proj_heads_kernel(h_ref, w_ref, *rest, epilogue):
    p_ref, (o_ref, wb_ref) = rest[0], rest[-2:]
    _cast_weights([w_ref], wb_ref)
    tm = h_ref.shape[0]
    rc = min(tm, PROJ_ROW_CHUNK)
    for r in range(tm // rc):
        rows = slice(r * rc, (r + 1) * rc)
        acc = jnp.dot(h_ref[rows, :], wb_ref[0], preferred_element_type=F32)
        for g in range(o_ref.shape[0]):
            cols = slice(g * LANE, (g + 1) * LANE)
            a = acc[:, cols]
            if epilogue == "norm":
                ms = jnp.mean(a * a, axis=-1, keepdims=True)
                a = a * lax.rsqrt(ms + EPS) * p_ref[...]
            elif epilogue == "gate":
                a = jax.nn.sigmoid(a + p_ref[:, cols])
            o_ref[g, rows, :] = a.astype(o_ref.dtype)


def _proj_heads(h, w, layer, segs, *, seg_cols, gains=None, bias=None):
    T, D = h.shape
    tm = _tile(T, 2048)
    tn = _tile(seg_cols, 1024)
    tps = seg_cols // tn
    n_cols = len(segs) * seg_cols

    def col_block(j):
        k = j // tps
        seg = sum(jnp.where(k == n, s, 0) for n, s in enumerate(segs))
        return seg * tps + j % tps

    if gains is not None:
        epilogue, p, p_spec = "norm", gains, pl.BlockSpec((None, 1, LANE), lambda j, i: (j // tps, 0, 0))
    elif bias is not None:
        epilogue, p, p_spec = "gate", bias, pl.BlockSpec((1, tn), lambda j, i: (0, col_block(j)))
    else:
        epilogue, p, p_spec = "plain", None, None
    return pl.pallas_call(
        functools.partial(_proj_heads_kernel, epilogue=epilogue),
        grid=(n_cols // tn, T // tm),
        in_specs=[pl.BlockSpec((tm, D), lambda j, i: (i, 0)),
                  pl.BlockSpec((None, D, tn), lambda j, i: (layer, 0, col_block(j)))]
                 + ([] if p is None else [p_spec]),
        out_specs=pl.BlockSpec((tn // LANE, tm, LANE), lambda j, i: (j, i, 0)),
        out_shape=jax.ShapeDtypeStruct((n_cols // LANE, T, LANE), BF16),
        scratch_shapes=[pltpu.VMEM((1, D, tn), BF16)],
        compiler_params=_params("arbitrary", "arbitrary"),
        name="proj_heads_" + epilogue,
    )(h, w, *([] if p is None else [p]))


def _proj_conv_kernel(h_ref, wbg_ref, wcg_ref, wxc_ref, wgc_ref, b_ref, cw_ref, *rest, n_cast):
    o_ref, wb_ref = rest[n_cast], rest[-1]
    for src_ref, dst_ref in zip(rest[:n_cast], rest[n_cast + 1:-1]):
        dst_ref[...] = src_ref[...].astype(dst_ref.dtype)
    _cast_weights([wbg_ref, wcg_ref, wxc_ref, wgc_ref], wb_ref)
    S = h_ref.shape[0]
    rc = min(S, PROJ_ROW_CHUNK)
    row = lax.broadcasted_iota(jnp.int32, (rc, 1), 0)
    tail = jnp.zeros((F32_SUBLANES, o_ref.shape[1]), F32)
    for r0 in range(0, S, rc):
        rows = slice(r0, r0 + rc)
        hc = h_ref[rows, :]
        bg, cg, xc, gc = [jnp.dot(hc, wb_ref[n], preferred_element_type=F32) for n in range(4)]
        u = cg * xc
        prev1, prev2 = tail[F32_SUBLANES - 1:, :], tail[F32_SUBLANES - 2:F32_SUBLANES - 1, :]
        u1 = jnp.where(row == 0, prev1, pltpu.roll(u, 1, 0))
        u2 = jnp.where(row == 0, prev2, jnp.where(row == 1, prev1, pltpu.roll(u, 2, 0)))
        conv = cw_ref[0:1, :] * u2 + cw_ref[1:2, :] * u1 + cw_ref[2:3, :] * u
        o_ref[rows, :] = (jax.nn.sigmoid(gc + b_ref[...]) * (bg * conv)).astype(o_ref.dtype)
        tail = u[rc - F32_SUBLANES:, :]


def _proj_conv(h, w_in, w_gate, layer, b_gate, conv_w, to_cast=(), *, seq, bg_col, cg_col, xc_col, gc_col):
    T, D = h.shape
    tn = _tile(D, CONV_COL_TILE)
    n_seq = T // seq
    steps = (D // tn) * n_seq

    def w_spec(col0):
        return pl.BlockSpec((None, D, tn), lambda j, i: (layer, 0, col0 // tn + j))

    slab_in, slab_out, slab_shape = [], [], []
    for w in to_cast:
        _, R, C = w.shape
        assert R % (steps * 2 * F32_SUBLANES) == 0, "row slabs must hold whole packed bf16 register tiles"
        slab_in.append(pl.BlockSpec((None, R // steps, C), lambda j, i: (layer, j * n_seq + i, 0)))
        slab_out.append(pl.BlockSpec((R // steps, C), lambda j, i: (j * n_seq + i, 0)))
        slab_shape.append(jax.ShapeDtypeStruct((R, C), BF16))
    out = pl.pallas_call(
        functools.partial(_proj_conv_kernel, n_cast=len(to_cast)),
        grid=(D // tn, n_seq),
        in_specs=[pl.BlockSpec((seq, D), lambda j, i: (i, 0)),
                  w_spec(bg_col), w_spec(cg_col), w_spec(xc_col), w_spec(gc_col),
                  pl.BlockSpec((1, tn), lambda j, i: (0, gc_col // tn + j)),
                  pl.BlockSpec((3, tn), lambda j, i: (0, j))] + slab_in,
        out_specs=[pl.BlockSpec((seq, tn), lambda j, i: (i, j))] + slab_out,
        out_shape=[jax.ShapeDtypeStruct((T, D), BF16)] + slab_shape,
        scratch_shapes=[pltpu.VMEM((4, D, tn), BF16)],
        compiler_params=_params("arbitrary", "arbitrary"),
        name="proj_conv",
    )(h, w_in, w_in, w_in, w_gate, b_gate.reshape(1, -1).astype(F32), conv_w.astype(F32), *to_cast)
    return out[0], out[1:]


def _attn_a_bias_tile(rel_bias):
    H = rel_bias.shape[0]
    table = rel_bias.astype(F32)
    n_far = A_LEFT_CHUNKS * CHUNK - A_MAX_REL + CHUNK - 1
    far = jnp.broadcast_to(table[:, -1:], (H, n_far))
    near = table[:, :A_MAX_REL - CHUNK:-1]
    f = jnp.concatenate([far, near, jnp.zeros((H, 1), F32)], axis=1)
    L = f.shape[1] - 1
    windows = jnp.tile(f, (1, CHUNK))[:, :CHUNK * L].reshape(H, CHUNK, L)
    t = windows[:, :, CHUNK - 1:]
    masked = jnp.full((H, CHUNK, CHUNK), MASKED, F32)
    t = t * LOG2E
    return jnp.concatenate([jnp.concatenate([t, masked], axis=2),
                            jnp.concatenate([masked, t], axis=2)], axis=1)


def _attn_a_steps(q_ref, k_ref, v_ref, bias_ref, gate_ref, o_ref, vx_ref):
    S = q_ref.shape[0]
    n_blocks = S // A_QBLOCK
    vx_ref[:, :HEAD_DIM] = v_ref[...]
    vx_ref[:, HEAD_DIM:] = jnp.ones((S, HEAD_DIM), vx_ref.dtype)

    def band(blk):
        k1 = (blk + 1) * A_QBLOCK
        return max(0, k1 - A_BAND), k1

    def scores(blk):
        k0, k1 = band(blk)
        b0 = A_BAND - (k1 - k0)
        s = lax.dot_general(q_ref[blk * A_QBLOCK:k1, :], k_ref[k0:k1, :], (((1,), (1,)), ((), ())),
                            preferred_element_type=F32)
        return s + bias_ref[:, b0:]

    s_next = scores(0)
    for blk in range(n_blocks):
        s = s_next
        if blk + 1 < n_blocks:
            s_next = scores(blk + 1)
        k0, k1 = band(blk)
        e = jnp.exp2(s - jnp.max(s, axis=-1, keepdims=True))
        ox = jnp.dot(e.astype(BF16), vx_ref[k0:k1, :], preferred_element_type=F32)
        rows = slice(blk * A_QBLOCK, k1)
        o_ref[rows, :] = (ox[:, :HEAD_DIM] / ox[:, HEAD_DIM:]).astype(o_ref.dtype) * gate_ref[rows, :]
        yield


def _attn_b_steps(lam_ref, q_ref, k_ref, v_ref, g_ref, gate_ref, o_ref, *, out_scale):
    S = q_ref.shape[1]
    lam = lam_ref[0]
    v = jnp.concatenate([v_ref[0], v_ref[1]], axis=-1)
    chunk = lax.broadcasted_iota(jnp.int32, (B_QBLOCK, HEAD_DIM), 0) // CHUNK
    lane = lax.broadcasted_iota(jnp.int32, (B_QBLOCK, HEAD_DIM), 1)
    q_hot = jnp.where(lane == chunk, 1.0, 0.0).astype(q_ref.dtype)
    k_hot = jnp.where((lane < B_QBLOCK // CHUNK) & (chunk > lane), MASKED, 0.0).astype(k_ref.dtype)
    gs = g_ref[...] * out_scale
    nt = (((1,), (1,)), ((), ()))

    def scores(blk):
        q0, q1 = blk * B_QBLOCK, (blk + 1) * B_QBLOCK
        out = []
        for m in range(2):
            q = q_ref[m, q0:q1, :]
            sd = lax.dot_general(jnp.concatenate([q, q_hot], axis=1),
                                 jnp.concatenate([k_ref[m, q0:q1, :], k_hot], axis=1), nt,
                                 preferred_element_type=F32)
            sl = lax.dot_general(q, k_ref[m, 0:q0, :], nt, preferred_element_type=F32) if blk else None
            out.append((sl, sd))
        return out

    def softmax_parts(sl, sd):
        mx = jnp.max(sd, axis=-1, keepdims=True)
        el = None
        if sl is not None:
            mx = jnp.maximum(mx, jnp.max(sl, axis=-1, keepdims=True))
            el = jnp.exp2(sl - mx)
        ed = jnp.exp2(sd - mx)
        l = jnp.sum(ed, axis=-1, keepdims=True)
        if sl is not None:
            l = l + jnp.sum(el, axis=-1, keepdims=True)
        return el, ed, l

    n_blocks = S // B_QBLOCK
    s_next = scores(0)
    for blk in range(n_blocks):
        q0, q1 = blk * B_QBLOCK, (blk + 1) * B_QBLOCK
        (sl1, sd1), (sl2, sd2) = s_next
        if blk + 1 < n_blocks:
            s_next = scores(blk + 1)
        yield
        el1, ed1, l1 = softmax_parts(sl1, sd1)
        el2, ed2, l2 = softmax_parts(sl2, sd2)
        r = lam * l1 / l2
        ob = jnp.dot((ed1 - ed2 * r).astype(BF16), v[q0:q1, :], preferred_element_type=F32)
        if blk:
            ob = ob + jnp.dot((el1 - el2 * r).astype(BF16), v[0:q0, :], preferred_element_type=F32)
        inv_l1 = 1.0 / l1
        ms = jnp.mean(ob * ob, axis=-1, keepdims=True) * (inv_l1 * inv_l1)
        y = (ob * (inv_l1 * lax.rsqrt(ms + EPS)) * gs).astype(o_ref.dtype)
        for half in range(2):
            cols = slice(half * HEAD_DIM, (half + 1) * HEAD_DIM)
            o_ref[q0:q1, cols] = y[:, cols] * gate_ref[half, q0:q1, :]
        yield


def _attn_kernel(lam_ref, qa_ref, ka_ref, va_ref, bias_ref, ga_ref, qb_ref, kb_ref, vb_ref, g_ref, gb_ref,
                 oa_ref, ob_ref, vx_ref, *, out_scale):
    S = qa_ref.shape[1]
    for i in range(ob_ref.shape[0]):
        pair = pl.ds(2 * i, 2)
        for _ in _attn_b_steps(lam_ref, qb_ref.at[pair], kb_ref.at[pair], vb_ref.at[pair], g_ref, gb_ref.at[pair],
                               ob_ref.at[i], out_scale=out_scale):
            pass
    a_progs = [_attn_a_steps(qa_ref.at[i], ka_ref.at[i], va_ref.at[i], bias_ref.at[i], ga_ref.at[i],
                             oa_ref.at[i], vx_ref.at[i]) for i in range(qa_ref.shape[0])]
    for _ in range(S // A_QBLOCK):
        for prog in a_progs:
            next(prog)


def _attention(qk, vv, gates, bias_tile, lam, subln_g, *, batch, seq, b_heads, out_scale):
    T = qk.shape[1]
    nb = math.gcd(ATTN_B_HEADS, b_heads)
    steps = b_heads // nb

    def groups(seg):
        return pl.BlockSpec((2 * nb, seq, HEAD_DIM), lambda b, h: (seg * steps + h, b, 0))

    return pl.pallas_call(
        functools.partial(_attn_kernel, out_scale=out_scale),
        grid=(batch, steps),
        in_specs=[pl.BlockSpec(memory_space=pltpu.SMEM),
                  groups(0), groups(1), groups(0),
                  pl.BlockSpec((2 * nb, A_QBLOCK, A_BAND), lambda b, h: (h, 0, 0)), groups(0),
                  groups(2), groups(3), groups(1),
                  pl.BlockSpec((1, 2 * HEAD_DIM), lambda b, h: (0, 0)), groups(1)],
        out_specs=[pl.BlockSpec((2 * nb, seq, HEAD_DIM), lambda b, h: (h, b, 0)),
                   pl.BlockSpec((nb, seq, 2 * HEAD_DIM), lambda b, h: (h, b, 0))],
        out_shape=[jax.ShapeDtypeStruct((2 * b_heads, T, HEAD_DIM), BF16),
                   jax.ShapeDtypeStruct((b_heads, T, 2 * HEAD_DIM), BF16)],
        scratch_shapes=[pltpu.VMEM((2 * nb, seq, 2 * HEAD_DIM), BF16)],
        compiler_params=_params("parallel", "parallel"),
        name="attention",
    )(lam, qk, qk, vv, bias_tile, gates, qk, qk, vv, subln_g.reshape(1, 2 * HEAD_DIM).astype(F32), gates)


def _merge_out_kernel(x_ref, oa_ref, ob_ref, oc_ref, wo_ref, g2_ref, x1_ref, h2_ref, m_ref):
    D = x_ref.shape[1]
    for g in range(D // LANE):
        sl = slice(g * LANE, (g + 1) * LANE)
        half = slice((g % 2) * LANE, (g % 2 + 1) * LANE)
        m_ref[:, sl] = oa_ref[g] + ob_ref[g // 2, :, half] + oc_ref[:, sl]
    x1 = x_ref[...] + jnp.dot(m_ref[...], wo_ref[...], preferred_element_type=F32)
    x1_ref[...] = x1
    ms = jnp.mean(x1 * x1, axis=-1, keepdims=True)
    h2_ref[...] = (x1 * lax.rsqrt(ms + EPS) * g2_ref[...]).astype(h2_ref.dtype)


def _merge_out(x, out_a, out_b, out_c, w_o, g2):
    T, D = x.shape
    tm = _tile(T, 512)
    row = lambda i: (i, 0)
    once = pl.Buffered(1)
    return pl.pallas_call(
        _merge_out_kernel,
        grid=(T // tm,),
        in_specs=[pl.BlockSpec((tm, D), row),
                  pl.BlockSpec((out_a.shape[0], tm, HEAD_DIM), lambda i: (0, i, 0)),
                  pl.BlockSpec((out_b.shape[0], tm, 2 * HEAD_DIM), lambda i: (0, i, 0)),
                  pl.BlockSpec((tm, D), row),
                  pl.BlockSpec((D, D), lambda i: (0, 0), pipeline_mode=once),
                  pl.BlockSpec((1, D), lambda i: (0, 0), pipeline_mode=once)],
        out_specs=[pl.BlockSpec((tm, D), row), pl.BlockSpec((tm, D), row)],
        out_shape=[jax.ShapeDtypeStruct((T, D), F32), jax.ShapeDtypeStruct((T, D), BF16)],
        scratch_shapes=[pltpu.VMEM((tm, D), BF16)],
        compiler_params=_params("parallel"),
        name="merge_out",
    )(x, out_a, out_b, out_c, w_o, g2.reshape(1, D).astype(F32))


def _mlp_kernel(x1_ref, h2_ref, w1_ref, w2_ref, *rest, with_next):
    x2_ref = rest[1] if with_next else rest[0]
    f = pl.program_id(1)
    tm = h2_ref.shape[0]
    rc = min(tm, PROJ_ROW_CHUNK)
    chunks = [slice(r * rc, (r + 1) * rc) for r in range(tm // rc)]

    @pl.when(f == 0)
    def _():
        x2_ref[...] = x1_ref[...]

    acts = [jnp.dot(h2_ref[rows, :], w1_ref[...], preferred_element_type=F32) for rows in chunks]
    for rows, a in zip(chunks, acts):
        a = jnp.square(jnp.maximum(a, 0.0)).astype(BF16)
        x2_ref[rows, :] += jnp.dot(a, w2_ref[...], preferred_element_type=F32)

    if with_next:
        @pl.when(f == pl.num_programs(1) - 1)
        def _():
            x2 = x2_ref[...]
            ms = jnp.mean(x2 * x2, axis=-1, keepdims=True)
            rest[2][...] = (x2 * lax.rsqrt(ms + EPS) * rest[0][...]).astype(BF16)


def _mlp_ff_tile(tm, D, FF, with_next):
    rc = min(tm, PROJ_ROW_CHUNK)
    rows = tm * D * (4 + 2 + 4 + (2 if with_next else 0))
    for tf in (2048, 1024, 512, 256, LANE):
        if FF % tf:
            continue
        weights = 2 * D * tf * 2
        temps = (tm // rc) * rc * tf * (4 + 2)
        if 2 * (rows + weights) + temps <= MLP_VMEM_BUDGET:
            return tf
    return _tile(FF, LANE)


def _mlp(x1, h2, w1, w2, g_next):
    T, D = x1.shape
    FF = w1.shape[1]
    with_next = g_next is not None
    tm = _tile(T, 512)
    tf = _mlp_ff_tile(tm, D, FF, with_next)
    row = lambda i, f: (i, 0)
    in_specs = [pl.BlockSpec((tm, D), row),
                pl.BlockSpec((tm, D), row),
                pl.BlockSpec((D, tf), lambda i, f: (0, f)),
                pl.BlockSpec((tf, D), lambda i, f: (f, 0))]
    args = [x1, h2, w1, w2]
    out_specs = [pl.BlockSpec((tm, D), row)]
    out_shape = [jax.ShapeDtypeStruct((T, D), F32)]
    if with_next:
        in_specs.append(pl.BlockSpec((1, D), lambda i, f: (0, 0)))
        args.append(g_next.reshape(1, D).astype(F32))
        out_specs.append(pl.BlockSpec((tm, D), row))
        out_shape.append(jax.ShapeDtypeStruct((T, D), BF16))
    out = pl.pallas_call(
        functools.partial(_mlp_kernel, with_next=with_next),
        grid=(T // tm, FF // tf),
        in_specs=in_specs,
        out_specs=out_specs,
        out_shape=out_shape,
        compiler_params=pltpu.CompilerParams(dimension_semantics=("parallel", "arbitrary"),
                                             vmem_limit_bytes=MLP_VMEM_LIMIT),
        name="mlp",
    )(*args)
    return (out[0], out[1]) if with_next else (out[0], None)


def kernel(x, norm1_g, w_in, w_gate, b_gate, a_qn_g, a_kn_g, a_rel_bias, b_qn_g, b_kn_g,
           b_lq1, b_lk1, b_lq2, b_lk2, b_subln_g, c_conv_w, w_o, norm2_g, w_mlp1, w_mlp2):
    B, S, D = x.shape
    depth = w_in.shape[0]
    T = B * S
    b_heads = D // (2 * HEAD_DIM)
    assert S % B_QBLOCK == 0 and D % (2 * HEAD_DIM) == 0 and w_in.shape[2] == 9 * D
    scale = HEAD_DIM ** -0.5 * LOG2E

    xf = x.reshape(T, D).astype(F32)
    w_in, w_gate = w_in.astype(F32), w_gate.astype(F32)
    w_o, w_mlp1, w_mlp2 = w_o.astype(F32), w_mlp1.astype(F32), w_mlp2.astype(F32)

    h = _rmsnorm(xf, norm1_g[0])
    for l in range(depth):
        gains = jnp.stack([a_qn_g[l].astype(F32) * scale, a_kn_g[l].astype(F32),
                           b_qn_g[l].astype(F32) * scale, b_kn_g[l].astype(F32)]).reshape(4, 1, HEAD_DIM)
        qk = _proj_heads(h, w_in, l, (0, 1, 3, 4), seg_cols=D, gains=gains)
        vv = _proj_heads(h, w_in, l, (2, 5), seg_cols=D)
        gates = _proj_heads(h, w_gate, l, (0, 1), seg_cols=D,
                            bias=b_gate[l].reshape(1, -1).astype(F32))
        out_c, (w_o_b, w1_b, w2_b) = _proj_conv(h, w_in, w_gate, l, b_gate[l], c_conv_w[l], (w_o, w_mlp1, w_mlp2),
                                                seq=S, bg_col=6 * D, cg_col=7 * D, xc_col=8 * D, gc_col=2 * D)

        lambda_init = 0.8 - 0.6 * math.exp(-0.3 * l)
        lam = (jnp.exp(jnp.sum(b_lq1[l].astype(F32) * b_lk1[l].astype(F32)))
               - jnp.exp(jnp.sum(b_lq2[l].astype(F32) * b_lk2[l].astype(F32))) + lambda_init).reshape(1)
        out_a, out_b = _attention(qk, vv, gates, _attn_a_bias_tile(a_rel_bias[l]), lam, b_subln_g[l],
                                  batch=B, seq=S, b_heads=b_heads, out_scale=1.0 - lambda_init)

        x1, h2 = _merge_out(xf, out_a, out_b, out_c, w_o_b, norm2_g[l])
        xf, h = _mlp(x1, h2, w1_b, w2_b, norm1_g[l + 1] if l + 1 < depth else None)
    return xf.reshape(B, S, D).astype(x.dtype)
```

```python
import functools
import math

import jax
import jax.numpy as jnp
from jax import lax
from jax.experimental import pallas as pl
from jax.experimental.pallas import tpu as pltpu

F32 = jnp.float32
BF16 = jnp.bfloat16

LANE = 128
F32_SUBLANES = 8
CHUNK = 64
HEAD_DIM = 128
EPS = 1e-6
A_LEFT_CHUNKS = 8
A_MAX_REL = 128
A_QBLOCK = 2 * CHUNK
A_BAND = (A_LEFT_CHUNKS + 2) * CHUNK
B_QBLOCK = 256
ATTN_B_HEADS = 1
PROJ_ROW_CHUNK = 256
CONV_COL_TILE = 256
MASKED = -1e30
LOG2E = math.log2(math.e)
VMEM_LIMIT = 56 * 1024 * 1024
MLP_VMEM_LIMIT = 62 * 1024 * 1024
MLP_VMEM_BUDGET = 60 * 1024 * 1024


def _params(*sem):
    return pltpu.CompilerParams(dimension_semantics=sem, vmem_limit_bytes=VMEM_LIMIT)


def _tile(n, pref):
    if n <= pref:
        return n
    t = pref
    while n % t:
        t -= LANE
    assert t > 0
    return t


def _rmsnorm_kernel(x_ref, g_ref, o_ref):
    x = x_ref[...]
    ms = jnp.mean(x * x, axis=-1, keepdims=True)
    o_ref[...] = (x * lax.rsqrt(ms + EPS) * g_ref[...]).astype(o_ref.dtype)


def _rmsnorm(x, g):
    T, D = x.shape
    tm = _tile(T, 512)
    return pl.pallas_call(
        _rmsnorm_kernel,
        grid=(T // tm,),
        in_specs=[pl.BlockSpec((tm, D), lambda i: (i, 0)),
                  pl.BlockSpec((1, D), lambda i: (0, 0))],
        out_specs=pl.BlockSpec((tm, D), lambda i: (i, 0)),
        out_shape=jax.ShapeDtypeStruct((T, D), BF16),
        compiler_params=_params("parallel"),
        name="rmsnorm",
    )(x, g.reshape(1, D).astype(F32))


def _cast_weights(w_refs, wb_ref):
    @pl.when(pl.program_id(1) == 0)
    def _():
        for n, w_ref in enumerate(w_refs):
            wb_ref[n] = w_ref[...].astype(wb_ref.dtype)


def _proj_heads_kernel(h_ref, w_ref, *rest, epilogue):
    p_ref, (o_ref, wb_ref) = rest[0], rest[-2:]
    _cast_weights([w_ref], wb_ref)
    tm = h_ref.shape[0]
    rc = min(tm, PROJ_ROW_CHUNK)
    for r in range(tm // rc):
        rows = slice(r * rc, (r + 1) * rc)
        acc = jnp.dot(h_ref[rows, :], wb_ref[0], preferred_element_type=F32)
        for g in range(o_ref.shape[0]):
            cols = slice(g * LANE, (g + 1) * LANE)
            a = acc[:, cols]
            if epilogue == "norm":
                ms = jnp.mean(a * a, axis=-1, keepdims=True)
                a = a * lax.rsqrt(ms + EPS) * p_ref[...]
            elif epilogue == "gate":
                a = jax.nn.sigmoid(a + p_ref[:, cols])
            o_ref[g, rows, :] = a.astype(o_ref.dtype)


def _proj_heads(h, w, layer, segs, *, seg_cols, gains=None, bias=None):
    T, D = h.shape
    tm = _tile(T, 2048)
    tn = _tile(seg_cols, 1024)
    tps = seg_cols // tn
    n_cols = len(segs) * seg_cols

    def col_block(j):
        k = j // tps
        seg = sum(jnp.where(k == n, s, 0) for n, s in enumerate(segs))
        return seg * tps + j % tps

    if gains is not None:
        epilogue, p, p_spec = "norm", gains, pl.BlockSpec((None, 1, LANE), lambda j, i: (j // tps, 0, 0))
    elif bias is not None:
        epilogue, p, p_spec = "gate", bias, pl.BlockSpec((1, tn), lambda j, i: (0, col_block(j)))
    else:
        epilogue, p, p_spec = "plain", None, None
    return pl.pallas_call(
        functools.partial(_proj_heads_kernel, epilogue=epilogue),
        grid=(n_cols // tn, T // tm),
        in_specs=[pl.BlockSpec((tm, D), lambda j, i: (i, 0)),
                  pl.BlockSpec((None, D, tn), lambda j, i: (layer, 0, col_block(j)))]
                 + ([] if p is None else [p_spec]),
        out_specs=pl.BlockSpec((tn // LANE, tm, LANE), lambda j, i: (j, i, 0)),
        out_shape=jax.ShapeDtypeStruct((n_cols // LANE, T, LANE), BF16),
        scratch_shapes=[pltpu.VMEM((1, D, tn), BF16)],
        compiler_params=_params("arbitrary", "arbitrary"),
        name="proj_heads_" + epilogue,
    )(h, w, *([] if p is None else [p]))


def _proj_conv_kernel(h_ref, wbg_ref, wcg_ref, wxc_ref, wgc_ref, b_ref, cw_ref, *rest, n_cast):
    o_ref, wb_ref = rest[n_cast], rest[-1]
    for src_ref, dst_ref in zip(rest[:n_cast], rest[n_cast + 1:-1]):
        dst_ref[...] = src_ref[...].astype(dst_ref.dtype)
    _cast_weights([wbg_ref, wcg_ref, wxc_ref, wgc_ref], wb_ref)
    S = h_ref.shape[0]
    rc = min(S, PROJ_ROW_CHUNK)
    row = lax.broadcasted_iota(jnp.int32, (rc, 1), 0)
    tail = jnp.zeros((F32_SUBLANES, o_ref.shape[1]), F32)
    for r0 in range(0, S, rc):
        rows = slice(r0, r0 + rc)
        hc = h_ref[rows, :]
        bg, cg, xc, gc = [jnp.dot(hc, wb_ref[n], preferred_element_type=F32) for n in range(4)]
        u = cg * xc
        prev1, prev2 = tail[F32_SUBLANES - 1:, :], tail[F32_SUBLANES - 2:F32_SUBLANES - 1, :]
        u1 = jnp.where(row == 0, prev1, pltpu.roll(u, 1, 0))
        u2 = jnp.where(row == 0, prev2, jnp.where(row == 1, prev1, pltpu.roll(u, 2, 0)))
        conv = cw_ref[0:1, :] * u2 + cw_ref[1:2, :] * u1 + cw_ref[2:3, :] * u
        o_ref[rows, :] = (jax.nn.sigmoid(gc + b_ref[...]) * (bg * conv)).astype(o_ref.dtype)
        tail = u[rc - F32_SUBLANES:, :]


def _proj_conv(h, w_in, w_gate, layer, b_gate, conv_w, to_cast=(), *, seq, bg_col, cg_col, xc_col, gc_col):
    T, D = h.shape
    tn = _tile(D, CONV_COL_TILE)
    n_seq = T // seq
    steps = (D // tn) * n_seq

    def w_spec(col0):
        return pl.BlockSpec((None, D, tn), lambda j, i: (layer, 0, col0 // tn + j))

    slab_in, slab_out, slab_shape = [], [], []
    for w in to_cast:
        _, R, C = w.shape
        assert R % (steps * 2 * F32_SUBLANES) == 0, "row slabs must hold whole packed bf16 register tiles"
        slab_in.append(pl.BlockSpec((None, R // steps, C), lambda j, i: (layer, j * n_seq + i, 0)))
        slab_out.append(pl.BlockSpec((R // steps, C), lambda j, i: (j * n_seq + i, 0)))
        slab_shape.append(jax.ShapeDtypeStruct((R, C), BF16))
    out = pl.pallas_call(
        functools.partial(_proj_conv_kernel, n_cast=len(to_cast)),
        grid=(D // tn, n_seq),
        in_specs=[pl.BlockSpec((seq, D), lambda j, i: (i, 0)),
                  w_spec(bg_col), w_spec(cg_col), w_spec(xc_col), w_spec(gc_col),
                  pl.BlockSpec((1, tn), lambda j, i: (0, gc_col // tn + j)),
                  pl.BlockSpec((3, tn), lambda j, i: (0, j))] + slab_in,
        out_specs=[pl.BlockSpec((seq, tn), lambda j, i: (i, j))] + slab_out,
        out_shape=[jax.ShapeDtypeStruct((T, D), BF16)] + slab_shape,
        scratch_shapes=[pltpu.VMEM((4, D, tn), BF16)],
        compiler_params=_params("arbitrary", "arbitrary"),
        name="proj_conv",
    )(h, w_in, w_in, w_in, w_gate, b_gate.reshape(1, -1).astype(F32), conv_w.astype(F32), *to_cast)
    return out[0], out[1:]


def _attn_a_bias_tile(rel_bias):
    H = rel_bias.shape[0]
    table = rel_bias.astype(F32)
    n_far = A_LEFT_CHUNKS * CHUNK - A_MAX_REL + CHUNK - 1
    far = jnp.broadcast_to(table[:, -1:], (H, n_far))
    near = table[:, :A_MAX_REL - CHUNK:-1]
    f = jnp.concatenate([far, near, jnp.zeros((H, 1), F32)], axis=1)
    L = f.shape[1] - 1
    windows = jnp.tile(f, (1, CHUNK))[:, :CHUNK * L].reshape(H, CHUNK, L)
    t = windows[:, :, CHUNK - 1:]
    masked = jnp.full((H, CHUNK, CHUNK), MASKED, F32)
    t = t * LOG2E
    return jnp.concatenate([jnp.concatenate([t, masked], axis=2),
                            jnp.concatenate([masked, t], axis=2)], axis=1)


def _attn_a_steps(q_ref, k_ref, v_ref, bias_ref, gate_ref, o_ref, vx_ref):
    S = q_ref.shape[0]
    n_blocks = S // A_QBLOCK
    vx_ref[:, :HEAD_DIM] = v_ref[...]
    vx_ref[:, HEAD_DIM:] = jnp.ones((S, HEAD_DIM), vx_ref.dtype)

    def band(blk):
        k1 = (blk + 1) * A_QBLOCK
        return max(0, k1 - A_BAND), k1

    def scores(blk):
        k0, k1 = band(blk)
        b0 = A_BAND - (k1 - k0)
        s = lax.dot_general(q_ref[blk * A_QBLOCK:k1, :], k_ref[k0:k1, :], (((1,), (1,)), ((), ())),
                            preferred_element_type=F32)
        return s + bias_ref[:, b0:]

    s_next = scores(0)
    for blk in range(n_blocks):
        s = s_next
        if blk + 1 < n_blocks:
            s_next = scores(blk + 1)
        k0, k1 = band(blk)
        e = jnp.exp2(s - jnp.max(s, axis=-1, keepdims=True))
        ox = jnp.dot(e.astype(BF16), vx_ref[k0:k1, :], preferred_element_type=F32)
        rows = slice(blk * A_QBLOCK, k1)
        o_ref[rows, :] = (ox[:, :HEAD_DIM] / ox[:, HEAD_DIM:]).astype(o_ref.dtype) * gate_ref[rows, :]
        yield


def _attn_b_steps(lam_ref, q_ref, k_ref, v_ref, g_ref, gate_ref, o_ref, *, out_scale):
    S = q_ref.shape[1]
    lam = lam_ref[0]
    v = jnp.concatenate([v_ref[0], v_ref[1]], axis=-1)
    chunk = lax.broadcasted_iota(jnp.int32, (B_QBLOCK, HEAD_DIM), 0) // CHUNK
    lane = lax.broadcasted_iota(jnp.int32, (B_QBLOCK, HEAD_DIM), 1)
    q_hot = jnp.where(lane == chunk, 1.0, 0.0).astype(q_ref.dtype)
    k_hot = jnp.where((lane < B_QBLOCK // CHUNK) & (chunk > lane), MASKED, 0.0).astype(k_ref.dtype)
    gs = g_ref[...] * out_scale
    nt = (((1,), (1,)), ((), ()))

    def scores(blk):
        q0, q1 = blk * B_QBLOCK, (blk + 1) * B_QBLOCK
        out = []
        for m in range(2):
            q = q_ref[m, q0:q1, :]
            sd = lax.dot_general(jnp.concatenate([q, q_hot], axis=1),
                                 jnp.concatenate([k_ref[m, q0:q1, :], k_hot], axis=1), nt,
                                 preferred_element_type=F32)
            sl = lax.dot_general(q, k_ref[m, 0:q0, :], nt, preferred_element_type=F32) if blk else None
            out.append((sl, sd))
        return out

    def softmax_parts(sl, sd):
        mx = jnp.max(sd, axis=-1, keepdims=True)
        el = None
        if sl is not None:
            mx = jnp.maximum(mx, jnp.max(sl, axis=-1, keepdims=True))
            el = jnp.exp2(sl - mx)
        ed = jnp.exp2(sd - mx)
        l = jnp.sum(ed, axis=-1, keepdims=True)
        if sl is not None:
            l = l + jnp.sum(el, axis=-1, keepdims=True)
        return el, ed, l

    n_blocks = S // B_QBLOCK
    s_next = scores(0)
    for blk in range(n_blocks):
        q0, q1 = blk * B_QBLOCK, (blk + 1) * B_QBLOCK
        (sl1, sd1), (sl2, sd2) = s_next
        if blk + 1 < n_blocks:
            s_next = scores(blk + 1)
        yield
        el1, ed1, l1 = softmax_parts(sl1, sd1)
        el2, ed2, l2 = softmax_parts(sl2, sd2)
        r = lam * l1 / l2
        ob = jnp.dot((ed1 - ed2 * r).astype(BF16), v[q0:q1, :], preferred_element_type=F32)
        if blk:
            ob = ob + jnp.dot((el1 - el2 * r).astype(BF16), v[0:q0, :], preferred_element_type=F32)
        inv_l1 = 1.0 / l1
        ms = jnp.mean(ob * ob, axis=-1, keepdims=True) * (inv_l1 * inv_l1)
        y = (ob * (inv_l1 * lax.rsqrt(ms + EPS)) * gs).astype(o_ref.dtype)
        for half in range(2):
            cols = slice(half * HEAD_DIM, (half + 1) * HEAD_DIM)
            o_ref[q0:q1, cols] = y[:, cols] * gate_ref[half, q0:q1, :]
        yield


def _attn_kernel(lam_ref, qa_ref, ka_ref, va_ref, bias_ref, ga_ref, qb_ref, kb_ref, vb_ref, g_ref, gb_ref,
                 oa_ref, ob_ref, vx_ref, *, out_scale):
    S = qa_ref.shape[1]
    for i in range(ob_ref.shape[0]):
        pair = pl.ds(2 * i, 2)
        for _ in _attn_b_steps(lam_ref, qb_ref.at[pair], kb_ref.at[pair], vb_ref.at[pair], g_ref, gb_ref.at[pair],
                               ob_ref.at[i], out_scale=out_scale):
            pass
    a_progs = [_attn_a_steps(qa_ref.at[i], ka_ref.at[i], va_ref.at[i], bias_ref.at[i], ga_ref.at[i],
                             oa_ref.at[i], vx_ref.at[i]) for i in range(qa_ref.shape[0])]
    for _ in range(S // A_QBLOCK):
        for prog in a_progs:
            next(prog)


def _attention(qk, vv, gates, bias_tile, lam, subln_g, *, batch, seq, b_heads, out_scale):
    T = qk.shape[1]
    nb = math.gcd(ATTN_B_HEADS, b_heads)
    steps = b_heads // nb

    def groups(seg):
        return pl.BlockSpec((2 * nb, seq, HEAD_DIM), lambda b, h: (seg * steps + h, b, 0))

    return pl.pallas_call(
        functools.partial(_attn_kernel, out_scale=out_scale),
        grid=(batch, steps),
        in_specs=[pl.BlockSpec(memory_space=pltpu.SMEM),
                  groups(0), groups(1), groups(0),
                  pl.BlockSpec((2 * nb, A_QBLOCK, A_BAND), lambda b, h: (h, 0, 0)), groups(0),
                  groups(2), groups(3), groups(1),
                  pl.BlockSpec((1, 2 * HEAD_DIM), lambda b, h: (0, 0)), groups(1)],
        out_specs=[pl.BlockSpec((2 * nb, seq, HEAD_DIM), lambda b, h: (h, b, 0)),
                   pl.BlockSpec((nb, seq, 2 * HEAD_DIM), lambda b, h: (h, b, 0))],
        out_shape=[jax.ShapeDtypeStruct((2 * b_heads, T, HEAD_DIM), BF16),
                   jax.ShapeDtypeStruct((b_heads, T, 2 * HEAD_DIM), BF16)],
        scratch_shapes=[pltpu.VMEM((2 * nb, seq, 2 * HEAD_DIM), BF16)],
        compiler_params=_params("parallel", "parallel"),
        name="attention",
    )(lam, qk, qk, vv, bias_tile, gates, qk, qk, vv, subln_g.reshape(1, 2 * HEAD_DIM).astype(F32), gates)


def _merge_out_kernel(x_ref, oa_ref, ob_ref, oc_ref, wo_ref, g2_ref, x1_ref, h2_ref, m_ref):
    D = x_ref.shape[1]
    for g in range(D // LANE):
        sl = slice(g * LANE, (g + 1) * LANE)
        half = slice((g % 2) * LANE, (g % 2 + 1) * LANE)
        m_ref[:, sl] = oa_ref[g] + ob_ref[g // 2, :, half] + oc_ref[:, sl]
    x1 = x_ref[...] + jnp.dot(m_ref[...], wo_ref[...], preferred_element_type=F32)
    x1_ref[...] = x1
    ms = jnp.mean(x1 * x1, axis=-1, keepdims=True)
    h2_ref[...] = (x1 * lax.rsqrt(ms + EPS) * g2_ref[...]).astype(h2_ref.dtype)


def _merge_out(x, out_a, out_b, out_c, w_o, g2):
    T, D = x.shape
    tm = _tile(T, 512)
    row = lambda i: (i, 0)
    once = pl.Buffered(1)
    return pl.pallas_call(
        _merge_out_kernel,
        grid=(T // tm,),
        in_specs=[pl.BlockSpec((tm, D), row),
                  pl.BlockSpec((out_a.shape[0], tm, HEAD_DIM), lambda i: (0, i, 0)),
                  pl.BlockSpec((out_b.shape[0], tm, 2 * HEAD_DIM), lambda i: (0, i, 0)),
                  pl.BlockSpec((tm, D), row),
                  pl.BlockSpec((D, D), lambda i: (0, 0), pipeline_mode=once),
                  pl.BlockSpec((1, D), lambda i: (0, 0), pipeline_mode=once)],
        out_specs=[pl.BlockSpec((tm, D), row), pl.BlockSpec((tm, D), row)],
        out_shape=[jax.ShapeDtypeStruct((T, D), F32), jax.ShapeDtypeStruct((T, D), BF16)],
        scratch_shapes=[pltpu.VMEM((tm, D), BF16)],
        compiler_params=_params("parallel"),
        name="merge_out",
    )(x, out_a, out_b, out_c, w_o, g2.reshape(1, D).astype(F32))


def _mlp_kernel(x1_ref, h2_ref, w1_ref, w2_ref, *rest, with_next):
    x2_ref = rest[1] if with_next else rest[0]
    f = pl.program_id(1)
    tm = h2_ref.shape[0]
    rc = min(tm, PROJ_ROW_CHUNK)
    chunks = [slice(r * rc, (r + 1) * rc) for r in range(tm // rc)]

    @pl.when(f == 0)
    def _():
        x2_ref[...] = x1_ref[...]

    for rows in chunks:
        a = jnp.dot(h2_ref[rows, :], w1_ref[...], preferred_element_type=F32)
        a = jnp.square(jnp.maximum(a, 0.0)).astype(BF16)
        x2_ref[rows, :] += jnp.dot(a, w2_ref[...], preferred_element_type=F32)

    if with_next:
        @pl.when(f == pl.num_programs(1) - 1)
        def _():
            x2 = x2_ref[...]
            ms = jnp.mean(x2 * x2, axis=-1, keepdims=True)
            rest[2][...] = (x2 * lax.rsqrt(ms + EPS) * rest[0][...]).astype(BF16)


def _mlp_ff_tile(tm, D, FF, with_next):
    rc = min(tm, PROJ_ROW_CHUNK)
    rows = tm * D * (4 + 2 + 4 + (2 if with_next else 0))
    for tf in (2048, 1024, 512, 256, LANE):
        if FF % tf:
            continue
        weights = 2 * D * tf * 2
        temps = rc * tf * (4 + 2)
        if 2 * (rows + weights) + temps <= MLP_VMEM_BUDGET:
            return tf
    return _tile(FF, LANE)


def _mlp(x1, h2, w1, w2, g_next):
    T, D = x1.shape
    FF = w1.shape[1]
    with_next = g_next is not None
    tm = _tile(T, 512)
    tf = _mlp_ff_tile(tm, D, FF, with_next)
    row = lambda i, f: (i, 0)
    in_specs = [pl.BlockSpec((tm, D), row),
                pl.BlockSpec((tm, D), row),
                pl.BlockSpec((D, tf), lambda i, f: (0, f)),
                pl.BlockSpec((tf, D), lambda i, f: (f, 0))]
    args = [x1, h2, w1, w2]
    out_specs = [pl.BlockSpec((tm, D), row)]
    out_shape = [jax.ShapeDtypeStruct((T, D), F32)]
    if with_next:
        in_specs.append(pl.BlockSpec((1, D), lambda i, f: (0, 0)))
        args.append(g_next.reshape(1, D).astype(F32))
        out_specs.append(pl.BlockSpec((tm, D), row))
        out_shape.append(jax.ShapeDtypeStruct((T, D), BF16))
    out = pl.pallas_call(
        functools.partial(_mlp_kernel, with_next=with_next),
        grid=(T // tm, FF // tf),
        in_specs=in_specs,
        out_specs=out_specs,
        out_shape=out_shape,
        compiler_params=pltpu.CompilerParams(dimension_semantics=("parallel", "arbitrary"),
                                             vmem_limit_bytes=MLP_VMEM_LIMIT),
        name="mlp",
    )(*args)
    return (out[0], out[1]) if with_next else (out[0], None)


def kernel(x, norm1_g, w_in, w_gate, b_gate, a_qn_g, a_kn_g, a_rel_bias, b_qn_g, b_kn_g,
           b_lq1, b_lk1, b_lq2, b_lk2, b_subln_g, c_conv_w, w_o, norm2_g, w_mlp1, w_mlp2):
    B, S, D = x.shape
    depth = w_in.shape[0]
    T = B * S
    b_heads = D // (2 * HEAD_DIM)
    assert S % B_QBLOCK == 0 and D % (2 * HEAD_DIM) == 0 and w_in.shape[2] == 9 * D
    scale = HEAD_DIM ** -0.5 * LOG2E

    xf = x.reshape(T, D).astype(F32)
    w_in, w_gate = w_in.astype(F32), w_gate.astype(F32)
    w_o, w_mlp1, w_mlp2 = w_o.astype(F32), w_mlp1.astype(F32), w_mlp2.astype(F32)

    h = _rmsnorm(xf, norm1_g[0])
    for l in range(depth):
        gains = jnp.stack([a_qn_g[l].astype(F32) * scale, a_kn_g[l].astype(F32),
                           b_qn_g[l].astype(F32) * scale, b_kn_g[l].astype(F32)]).reshape(4, 1, HEAD_DIM)
        qk = _proj_heads(h, w_in, l, (0, 1, 3, 4), seg_cols=D, gains=gains)
        vv = _proj_heads(h, w_in, l, (2, 5), seg_cols=D)
        gates = _proj_heads(h, w_gate, l, (0, 1), seg_cols=D,
                            bias=b_gate[l].reshape(1, -1).astype(F32))
        out_c, (w_o_b, w1_b, w2_b) = _proj_conv(h, w_in, w_gate, l, b_gate[l], c_conv_w[l], (w_o, w_mlp1, w_mlp2),
                                                seq=S, bg_col=6 * D, cg_col=7 * D, xc_col=8 * D, gc_col=2 * D)

        lambda_init = 0.8 - 0.6 * math.exp(-0.3 * l)
        lam = (jnp.exp(jnp.sum(b_lq1[l].astype(F32) * b_lk1[l].astype(F32)))
               - jnp.exp(jnp.sum(b_lq2[l].astype(F32) * b_lk2[l].astype(F32))) + lambda_init).reshape(1)
        out_a, out_b = _attention(qk, vv, gates, _attn_a_bias_tile(a_rel_bias[l]), lam, b_subln_g[l],
                                  batch=B, seq=S, b_heads=b_heads, out_scale=1.0 - lambda_init)

        x1, h2 = _merge_out(xf, out_a, out_b, out_c, w_o_b, norm2_g[l])
        xf, h = _mlp(x1, h2, w1_b, w2_b, norm1_g[l + 1] if l + 1 < depth else None)
    return xf.reshape(B, S, D).astype(x.dtype)
```

```python
import functools
import math

import jax
import jax.numpy as jnp
from jax import lax
from jax.experimental import pallas as pl
from jax.experimental.pallas import tpu as pltpu

F32 = jnp.float32
BF16 = jnp.bfloat16

LANE = 128
F32_SUBLANES = 8
CHUNK = 64
HEAD_DIM = 128
EPS = 1e-6
A_LEFT_CHUNKS = 8
A_MAX_REL = 128
A_QBLOCK = 2 * CHUNK
A_BAND = (A_LEFT_CHUNKS + 2) * CHUNK
B_QBLOCK = 256
ATTN_B_HEADS = 1
PROJ_ROW_CHUNK = 256
CONV_COL_TILE = 256
MASKED = -1e30
LOG2E = math.log2(math.e)
VMEM_LIMIT = 56 * 1024 * 1024
MLP_VMEM_LIMIT = 62 * 1024 * 1024
MLP_VMEM_BUDGET = 60 * 1024 * 1024


def _params(*sem):
    return pltpu.CompilerParams(dimension_semantics=sem, vmem_limit_bytes=VMEM_LIMIT)


def _tile(n, pref):
    if n <= pref:
        return n
    t = pref
    while n % t:
        t -= LANE
    assert t > 0
    return t


def _rmsnorm_kernel(x_ref, g_ref, o_ref):
    x = x_ref[...]
    ms = jnp.mean(x * x, axis=-1, keepdims=True)
    o_ref[...] = (x * lax.rsqrt(ms + EPS) * g_ref[...]).astype(o_ref.dtype)


def _rmsnorm(x, g):
    T, D = x.shape
    tm = _tile(T, 512)
    return pl.pallas_call(
        _rmsnorm_kernel,
        grid=(T // tm,),
        in_specs=[pl.BlockSpec((tm, D), lambda i: (i, 0)),
                  pl.BlockSpec((1, D), lambda i: (0, 0))],
        out_specs=pl.BlockSpec((tm, D), lambda i: (i, 0)),
        out_shape=jax.ShapeDtypeStruct((T, D), BF16),
        compiler_params=_params("parallel"),
        name="rmsnorm",
    )(x, g.reshape(1, D).astype(F32))


def _cast_weights(w_refs, wb_ref):
    @pl.when(pl.program_id(1) == 0)
    def _():
        for n, w_ref in enumerate(w_refs):
            wb_ref[n] = w_ref[...].astype(wb_ref.dtype)


def _proj_heads_kernel(h_ref, w_ref, *rest, epilogue, precast):
    p_ref = rest[0]
    if precast:
        o_ref, w_bf16 = rest[-1], w_ref
    else:
        o_ref, wb_ref = rest[-2:]
        _cast_weights([w_ref], wb_ref)
        w_bf16 = wb_ref.at[0]
    tm = h_ref.shape[0]
    rc = min(tm, PROJ_ROW_CHUNK)
    for r in range(tm // rc):
        rows = slice(r * rc, (r + 1) * rc)
        acc = jnp.dot(h_ref[rows, :], w_bf16[...], preferred_element_type=F32)
        for g in range(o_ref.shape[0]):
            cols = slice(g * LANE, (g + 1) * LANE)
            a = acc[:, cols]
            if epilogue == "norm":
                ms = jnp.mean(a * a, axis=-1, keepdims=True)
                a = a * lax.rsqrt(ms + EPS) * p_ref[...]
            elif epilogue == "gate":
                a = jax.nn.sigmoid(a + p_ref[:, cols])
            o_ref[g, rows, :] = a.astype(o_ref.dtype)


def _proj_heads(h, w, layer, segs, *, seg_cols, gains=None, bias=None):
    T, D = h.shape
    precast = w.ndim == 2
    tm = _tile(T, 2048)
    tn = _tile(seg_cols, 2048 if precast else 1024)
    tps = seg_cols // tn
    n_cols = len(segs) * seg_cols

    def col_block(j):
        k = j // tps
        seg = sum(jnp.where(k == n, s, 0) for n, s in enumerate(segs))
        return seg * tps + j % tps

    if gains is not None:
        epilogue, p, p_spec = "norm", gains, pl.BlockSpec((None, 1, LANE), lambda j, i: (j // tps, 0, 0))
    elif bias is not None:
        epilogue, p, p_spec = "gate", bias, pl.BlockSpec((1, tn), lambda j, i: (0, col_block(j)))
    else:
        epilogue, p, p_spec = "plain", None, None
    if precast:
        w_spec = pl.BlockSpec((D, tn), lambda j, i: (0, col_block(j)))
    else:
        w_spec = pl.BlockSpec((None, D, tn), lambda j, i: (layer, 0, col_block(j)))
    return pl.pallas_call(
        functools.partial(_proj_heads_kernel, epilogue=epilogue, precast=precast),
        grid=(n_cols // tn, T // tm),
        in_specs=[pl.BlockSpec((tm, D), lambda j, i: (i, 0)), w_spec] + ([] if p is None else [p_spec]),
        out_specs=pl.BlockSpec((tn // LANE, tm, LANE), lambda j, i: (j, i, 0)),
        out_shape=jax.ShapeDtypeStruct((n_cols // LANE, T, LANE), BF16),
        scratch_shapes=[] if precast else [pltpu.VMEM((1, D, tn), BF16)],
        compiler_params=_params("arbitrary", "arbitrary"),
        name="proj_heads_" + epilogue,
    )(h, w, *([] if p is None else [p]))


def _proj_conv_kernel(h_ref, wbg_ref, wcg_ref, wxc_ref, wgc_ref, b_ref, cw_ref, *rest, cast_plan):
    n_src = sum(cast_plan)
    o_ref, wb_ref = rest[n_src], rest[-1]
    srcs = iter(rest[:n_src])
    for dst_ref, n in zip(rest[n_src + 1:-1], cast_plan):
        width = dst_ref.shape[1] // n
        for k in range(n):
            dst_ref[:, k * width:(k + 1) * width] = next(srcs)[...].astype(dst_ref.dtype)
    _cast_weights([wbg_ref, wcg_ref, wxc_ref, wgc_ref], wb_ref)
    S = h_ref.shape[0]
    rc = min(S, PROJ_ROW_CHUNK)
    row = lax.broadcasted_iota(jnp.int32, (rc, 1), 0)
    tail = jnp.zeros((F32_SUBLANES, o_ref.shape[1]), F32)
    for r0 in range(0, S, rc):
        rows = slice(r0, r0 + rc)
        hc = h_ref[rows, :]
        bg, cg, xc, gc = [jnp.dot(hc, wb_ref[n], preferred_element_type=F32) for n in range(4)]
        u = cg * xc
        prev1, prev2 = tail[F32_SUBLANES - 1:, :], tail[F32_SUBLANES - 2:F32_SUBLANES - 1, :]
        u1 = jnp.where(row == 0, prev1, pltpu.roll(u, 1, 0))
        u2 = jnp.where(row == 0, prev2, jnp.where(row == 1, prev1, pltpu.roll(u, 2, 0)))
        conv = cw_ref[0:1, :] * u2 + cw_ref[1:2, :] * u1 + cw_ref[2:3, :] * u
        o_ref[rows, :] = (jax.nn.sigmoid(gc + b_ref[...]) * (bg * conv)).astype(o_ref.dtype)
        tail = u[rc - F32_SUBLANES:, :]


def _proj_conv(h, w_in, w_gate, layer, b_gate, conv_w, to_cast=(), *, seq, bg_col, cg_col, xc_col, gc_col):
    T, D = h.shape
    tn = _tile(D, CONV_COL_TILE)
    n_seq = T // seq
    steps = (D // tn) * n_seq

    def w_spec(col0):
        return pl.BlockSpec((None, D, tn), lambda j, i: (layer, 0, col0 // tn + j))

    slab_in, slab_args, slab_out, slab_shape = [], [], [], []
    for sources in to_cast:
        R, n_cols = sources[0][0].shape[1], sources[0][2]
        assert R % (steps * 2 * F32_SUBLANES) == 0, "row slabs must hold whole packed bf16 register tiles"
        for w, col0, nc in sources:
            assert w.shape[1] == R and nc == n_cols and col0 % nc == 0
            slab_in.append(pl.BlockSpec((None, R // steps, nc),
                                        lambda j, i, c=col0 // nc: (layer, j * n_seq + i, c)))
            slab_args.append(w)
        slab_out.append(pl.BlockSpec((R // steps, n_cols * len(sources)), lambda j, i: (j * n_seq + i, 0)))
        slab_shape.append(jax.ShapeDtypeStruct((R, n_cols * len(sources)), BF16))
    out = pl.pallas_call(
        functools.partial(_proj_conv_kernel, cast_plan=tuple(len(sources) for sources in to_cast)),
        grid=(D // tn, n_seq),
        in_specs=[pl.BlockSpec((seq, D), lambda j, i: (i, 0)),
                  w_spec(bg_col), w_spec(cg_col), w_spec(xc_col), w_spec(gc_col),
                  pl.BlockSpec((1, tn), lambda j, i: (0, gc_col // tn + j)),
                  pl.BlockSpec((3, tn), lambda j, i: (0, j))] + slab_in,
        out_specs=[pl.BlockSpec((seq, tn), lambda j, i: (i, j))] + slab_out,
        out_shape=[jax.ShapeDtypeStruct((T, D), BF16)] + slab_shape,
        scratch_shapes=[pltpu.VMEM((4, D, tn), BF16)],
        compiler_params=_params("arbitrary", "arbitrary"),
        name="proj_conv",
    )(h, w_in, w_in, w_in, w_gate, b_gate.reshape(1, -1).astype(F32), conv_w.astype(F32), *slab_args)
    return out[0], out[1:]


def _attn_a_bias_tile(rel_bias):
    H = rel_bias.shape[0]
    table = rel_bias.astype(F32)
    n_far = A_LEFT_CHUNKS * CHUNK - A_MAX_REL + CHUNK - 1
    far = jnp.broadcast_to(table[:, -1:], (H, n_far))
    near = table[:, :A_MAX_REL - CHUNK:-1]
    f = jnp.concatenate([far, near, jnp.zeros((H, 1), F32)], axis=1)
    L = f.shape[1] - 1
    windows = jnp.tile(f, (1, CHUNK))[:, :CHUNK * L].reshape(H, CHUNK, L)
    t = windows[:, :, CHUNK - 1:]
    masked = jnp.full((H, CHUNK, CHUNK), MASKED, F32)
    t = t * LOG2E
    return jnp.concatenate([jnp.concatenate([t, masked], axis=2),
                            jnp.concatenate([masked, t], axis=2)], axis=1)


def _attn_a_steps(q_ref, k_ref, v_ref, bias_ref, gate_ref, o_ref, vx_ref):
    S = q_ref.shape[0]
    n_blocks = S // A_QBLOCK
    vx_ref[:, :HEAD_DIM] = v_ref[...]
    vx_ref[:, HEAD_DIM:] = jnp.ones((S, HEAD_DIM), vx_ref.dtype)

    def band(blk):
        k1 = (blk + 1) * A_QBLOCK
        return max(0, k1 - A_BAND), k1

    def scores(blk):
        k0, k1 = band(blk)
        b0 = A_BAND - (k1 - k0)
        s = lax.dot_general(q_ref[blk * A_QBLOCK:k1, :], k_ref[k0:k1, :], (((1,), (1,)), ((), ())),
                            preferred_element_type=F32)
        return s + bias_ref[:, b0:]

    s_next = scores(0)
    for blk in range(n_blocks):
        s = s_next
        if blk + 1 < n_blocks:
            s_next = scores(blk + 1)
        k0, k1 = band(blk)
        e = jnp.exp2(s - jnp.max(s, axis=-1, keepdims=True))
        ox = jnp.dot(e.astype(BF16), vx_ref[k0:k1, :], preferred_element_type=F32)
        rows = slice(blk * A_QBLOCK, k1)
        o_ref[rows, :] = (ox[:, :HEAD_DIM] / ox[:, HEAD_DIM:]).astype(o_ref.dtype) * gate_ref[rows, :]
        yield


def _attn_b_steps(lam_ref, q_ref, k_ref, v_ref, g_ref, gate_ref, o_ref, *, out_scale):
    S = q_ref.shape[1]
    lam = lam_ref[0]
    v = jnp.concatenate([v_ref[0], v_ref[1]], axis=-1)
    chunk = lax.broadcasted_iota(jnp.int32, (B_QBLOCK, HEAD_DIM), 0) // CHUNK
    lane = lax.broadcasted_iota(jnp.int32, (B_QBLOCK, HEAD_DIM), 1)
    q_hot = jnp.where(lane == chunk, 1.0, 0.0).astype(q_ref.dtype)
    k_hot = jnp.where((lane < B_QBLOCK // CHUNK) & (chunk > lane), MASKED, 0.0).astype(k_ref.dtype)
    gs = g_ref[...] * out_scale
    nt = (((1,), (1,)), ((), ()))

    def scores(blk):
        q0, q1 = blk * B_QBLOCK, (blk + 1) * B_QBLOCK
        out = []
        for m in range(2):
            q = q_ref[m, q0:q1, :]
            sd = lax.dot_general(jnp.concatenate([q, q_hot], axis=1),
                                 jnp.concatenate([k_ref[m, q0:q1, :], k_hot], axis=1), nt,
                                 preferred_element_type=F32)
            sl = lax.dot_general(q, k_ref[m, 0:q0, :], nt, preferred_element_type=F32) if blk else None
            out.append((sl, sd))
        return out

    def softmax_parts(sl, sd):
        mx = jnp.max(sd, axis=-1, keepdims=True)
        el = None
        if sl is not None:
            mx = jnp.maximum(mx, jnp.max(sl, axis=-1, keepdims=True))
            el = jnp.exp2(sl - mx)
        ed = jnp.exp2(sd - mx)
        l = jnp.sum(ed, axis=-1, keepdims=True)
        if sl is not None:
            l = l + jnp.sum(el, axis=-1, keepdims=True)
        return el, ed, l

    n_blocks = S // B_QBLOCK
    s_next = scores(0)
    for blk in range(n_blocks):
        q0, q1 = blk * B_QBLOCK, (blk + 1) * B_QBLOCK
        (sl1, sd1), (sl2, sd2) = s_next
        if blk + 1 < n_blocks:
            s_next = scores(blk + 1)
        yield
        el1, ed1, l1 = softmax_parts(sl1, sd1)
        el2, ed2, l2 = softmax_parts(sl2, sd2)
        r = lam * l1 / l2
        ob = jnp.dot((ed1 - ed2 * r).astype(BF16), v[q0:q1, :], preferred_element_type=F32)
        if blk:
            ob = ob + jnp.dot((el1 - el2 * r).astype(BF16), v[0:q0, :], preferred_element_type=F32)
        inv_l1 = 1.0 / l1
        ms = jnp.mean(ob * ob, axis=-1, keepdims=True) * (inv_l1 * inv_l1)
        y = (ob * (inv_l1 * lax.rsqrt(ms + EPS)) * gs).astype(o_ref.dtype)
        for half in range(2):
            cols = slice(half * HEAD_DIM, (half + 1) * HEAD_DIM)
            o_ref[q0:q1, cols] = y[:, cols] * gate_ref[half, q0:q1, :]
        yield


def _attn_kernel(lam_ref, qa_ref, ka_ref, va_ref, bias_ref, ga_ref, qb_ref, kb_ref, vb_ref, g_ref, gb_ref,
                 oa_ref, ob_ref, vx_ref, *, out_scale):
    S = qa_ref.shape[1]
    for i in range(ob_ref.shape[0]):
        pair = pl.ds(2 * i, 2)
        for _ in _attn_b_steps(lam_ref, qb_ref.at[pair], kb_ref.at[pair], vb_ref.at[pair], g_ref, gb_ref.at[pair],
                               ob_ref.at[i], out_scale=out_scale):
            pass
    a_progs = [_attn_a_steps(qa_ref.at[i], ka_ref.at[i], va_ref.at[i], bias_ref.at[i], ga_ref.at[i],
                             oa_ref.at[i], vx_ref.at[i]) for i in range(qa_ref.shape[0])]
    for _ in range(S // A_QBLOCK):
        for prog in a_progs:
            next(prog)


def _attention(qk, vv, gates, bias_tile, lam, subln_g, *, batch, seq, b_heads, out_scale):
    T = qk.shape[1]
    nb = math.gcd(ATTN_B_HEADS, b_heads)
    steps = b_heads // nb

    def groups(seg):
        return pl.BlockSpec((2 * nb, seq, HEAD_DIM), lambda b, h: (seg * steps + h, b, 0))

    return pl.pallas_call(
        functools.partial(_attn_kernel, out_scale=out_scale),
        grid=(batch, steps),
        in_specs=[pl.BlockSpec(memory_space=pltpu.SMEM),
                  groups(0), groups(1), groups(0),
                  pl.BlockSpec((2 * nb, A_QBLOCK, A_BAND), lambda b, h: (h, 0, 0)), groups(0),
                  groups(2), groups(3), groups(1),
                  pl.BlockSpec((1, 2 * HEAD_DIM), lambda b, h: (0, 0)), groups(1)],
        out_specs=[pl.BlockSpec((2 * nb, seq, HEAD_DIM), lambda b, h: (h, b, 0)),
                   pl.BlockSpec((nb, seq, 2 * HEAD_DIM), lambda b, h: (h, b, 0))],
        out_shape=[jax.ShapeDtypeStruct((2 * b_heads, T, HEAD_DIM), BF16),
                   jax.ShapeDtypeStruct((b_heads, T, 2 * HEAD_DIM), BF16)],
        scratch_shapes=[pltpu.VMEM((2 * nb, seq, 2 * HEAD_DIM), BF16)],
        compiler_params=_params("parallel", "parallel"),
        name="attention",
    )(lam, qk, qk, vv, bias_tile, gates, qk, qk, vv, subln_g.reshape(1, 2 * HEAD_DIM).astype(F32), gates)


def _merge_out_kernel(x_ref, oa_ref, ob_ref, oc_ref, wo_ref, g2_ref, x1_ref, h2_ref, m_ref):
    D = x_ref.shape[1]
    for g in range(D // LANE):
        sl = slice(g * LANE, (g + 1) * LANE)
        half = slice((g % 2) * LANE, (g % 2 + 1) * LANE)
        m_ref[:, sl] = oa_ref[g] + ob_ref[g // 2, :, half] + oc_ref[:, sl]
    x1 = x_ref[...] + jnp.dot(m_ref[...], wo_ref[...], preferred_element_type=F32)
    x1_ref[...] = x1
    ms = jnp.mean(x1 * x1, axis=-1, keepdims=True)
    h2_ref[...] = (x1 * lax.rsqrt(ms + EPS) * g2_ref[...]).astype(h2_ref.dtype)


def _merge_out(x, out_a, out_b, out_c, w_o, g2):
    T, D = x.shape
    tm = _tile(T, 512)
    row = lambda i: (i, 0)
    once = pl.Buffered(1)
    return pl.pallas_call(
        _merge_out_kernel,
        grid=(T // tm,),
        in_specs=[pl.BlockSpec((tm, D), row),
                  pl.BlockSpec((out_a.shape[0], tm, HEAD_DIM), lambda i: (0, i, 0)),
                  pl.BlockSpec((out_b.shape[0], tm, 2 * HEAD_DIM), lambda i: (0, i, 0)),
                  pl.BlockSpec((tm, D), row),
                  pl.BlockSpec((D, D), lambda i: (0, 0), pipeline_mode=once),
                  pl.BlockSpec((1, D), lambda i: (0, 0), pipeline_mode=once)],
        out_specs=[pl.BlockSpec((tm, D), row), pl.BlockSpec((tm, D), row)],
        out_shape=[jax.ShapeDtypeStruct((T, D), F32), jax.ShapeDtypeStruct((T, D), BF16)],
        scratch_shapes=[pltpu.VMEM((tm, D), BF16)],
        compiler_params=_params("parallel"),
        name="merge_out",
    )(x, out_a, out_b, out_c, w_o, g2.reshape(1, D).astype(F32))


def _mlp_kernel(x1_ref, h2_ref, w1_ref, w2_ref, *rest, with_next):
    x2_ref = rest[1] if with_next else rest[0]
    f = pl.program_id(1)
    tm = h2_ref.shape[0]
    rc = min(tm, PROJ_ROW_CHUNK)
    chunks = [slice(r * rc, (r + 1) * rc) for r in range(tm // rc)]

    @pl.when(f == 0)
    def _():
        x2_ref[...] = x1_ref[...]

    for rows in chunks:
        a = jnp.dot(h2_ref[rows, :], w1_ref[...], preferred_element_type=F32)
        a = jnp.square(jnp.maximum(a, 0.0)).astype(BF16)
        x2_ref[rows, :] += jnp.dot(a, w2_ref[...], preferred_element_type=F32)

    if with_next:
        @pl.when(f == pl.num_programs(1) - 1)
        def _():
            x2 = x2_ref[...]
            ms = jnp.mean(x2 * x2, axis=-1, keepdims=True)
            rest[2][...] = (x2 * lax.rsqrt(ms + EPS) * rest[0][...]).astype(BF16)


def _mlp_ff_tile(tm, D, FF, with_next):
    rc = min(tm, PROJ_ROW_CHUNK)
    rows = tm * D * (4 + 2 + 4 + (2 if with_next else 0))
    for tf in (2048, 1024, 512, 256, LANE):
        if FF % tf:
            continue
        weights = 2 * D * tf * 2
        temps = rc * tf * (4 + 2)
        if 2 * (rows + weights) + temps <= MLP_VMEM_BUDGET:
            return tf
    return _tile(FF, LANE)


def _mlp(x1, h2, w1, w2, g_next):
    T, D = x1.shape
    FF = w1.shape[1]
    with_next = g_next is not None
    tm = _tile(T, 512)
    tf = _mlp_ff_tile(tm, D, FF, with_next)
    row = lambda i, f: (i, 0)
    in_specs = [pl.BlockSpec((tm, D), row),
                pl.BlockSpec((tm, D), row),
                pl.BlockSpec((D, tf), lambda i, f: (0, f)),
                pl.BlockSpec((tf, D), lambda i, f: (f, 0))]
    args = [x1, h2, w1, w2]
    out_specs = [pl.BlockSpec((tm, D), row)]
    out_shape = [jax.ShapeDtypeStruct((T, D), F32)]
    if with_next:
        in_specs.append(pl.BlockSpec((1, D), lambda i, f: (0, 0)))
        args.append(g_next.reshape(1, D).astype(F32))
        out_specs.append(pl.BlockSpec((tm, D), row))
        out_shape.append(jax.ShapeDtypeStruct((T, D), BF16))
    out = pl.pallas_call(
        functools.partial(_mlp_kernel, with_next=with_next),
        grid=(T // tm, FF // tf),
        in_specs=in_specs,
        out_specs=out_specs,
        out_shape=out_shape,
        compiler_params=pltpu.CompilerParams(dimension_semantics=("parallel", "arbitrary"),
                                             vmem_limit_bytes=MLP_VMEM_LIMIT),
        name="mlp",
    )(*args)
    return (out[0], out[1]) if with_next else (out[0], None)


def kernel(x, norm1_g, w_in, w_gate, b_gate, a_qn_g, a_kn_g, a_rel_bias, b_qn_g, b_kn_g,
           b_lq1, b_lk1, b_lq2, b_lk2, b_subln_g, c_conv_w, w_o, norm2_g, w_mlp1, w_mlp2):
    B, S, D = x.shape
    depth = w_in.shape[0]
    T = B * S
    b_heads = D // (2 * HEAD_DIM)
    assert S % B_QBLOCK == 0 and D % (2 * HEAD_DIM) == 0 and w_in.shape[2] == 9 * D
    scale = HEAD_DIM ** -0.5 * LOG2E

    xf = x.reshape(T, D).astype(F32)
    w_in, w_gate = w_in.astype(F32), w_gate.astype(F32)
    w_o, w_mlp1, w_mlp2 = w_o.astype(F32), w_mlp1.astype(F32), w_mlp2.astype(F32)

    h = _rmsnorm(xf, norm1_g[0])
    for l in range(depth):
        gains = jnp.stack([a_qn_g[l].astype(F32) * scale, a_kn_g[l].astype(F32),
                           b_qn_g[l].astype(F32) * scale, b_kn_g[l].astype(F32)]).reshape(4, 1, HEAD_DIM)
        qk = _proj_heads(h, w_in, l, (0, 1, 3, 4), seg_cols=D, gains=gains)
        FF = w_mlp1.shape[2]
        out_c, (w_o_b, w1_b, w2_b, wv_b, wg_b) = _proj_conv(
            h, w_in, w_gate, l, b_gate[l], c_conv_w[l],
            [[(w_o, 0, D)], [(w_mlp1, 0, FF)], [(w_mlp2, 0, D)],
             [(w_in, 2 * D, D), (w_in, 5 * D, D)], [(w_gate, 0, 2 * D)]],
            seq=S, bg_col=6 * D, cg_col=7 * D, xc_col=8 * D, gc_col=2 * D)
        vv = _proj_heads(h, wv_b, l, (0,), seg_cols=2 * D)
        gates = _proj_heads(h, wg_b, l, (0,), seg_cols=2 * D,
                            bias=b_gate[l].reshape(1, -1).astype(F32))

        lambda_init = 0.8 - 0.6 * math.exp(-0.3 * l)
        lam = (jnp.exp(jnp.sum(b_lq1[l].astype(F32) * b_lk1[l].astype(F32)))
               - jnp.exp(jnp.sum(b_lq2[l].astype(F32) * b_lk2[l].astype(F32))) + lambda_init).reshape(1)
        out_a, out_b = _attention(qk, vv, gates, _attn_a_bias_tile(a_rel_bias[l]), lam, b_subln_g[l],
                                  batch=B, seq=S, b_heads=b_heads, out_scale=1.0 - lambda_init)

        x1, h2 = _merge_out(xf, out_a, out_b, out_c, w_o_b, norm2_g[l])
        xf, h = _mlp(x1, h2, w1_b, w2_b, norm1_g[l + 1] if l + 1 < depth else None)
    return xf.reshape(B, S, D).astype(x.dtype)
```

```python
import functools
import math

import jax
import jax.numpy as jnp
from jax import lax
from jax.experimental import pallas as pl
from jax.experimental.pallas import tpu as pltpu

F32 = jnp.float32
BF16 = jnp.bfloat16

LANE = 128
F32_SUBLANES = 8
CHUNK = 64
HEAD_DIM = 128
EPS = 1e-6
A_LEFT_CHUNKS = 8
A_MAX_REL = 128
A_QBLOCK = 2 * CHUNK
A_BAND = (A_LEFT_CHUNKS + 2) * CHUNK
B_QBLOCK = 256
ATTN_B_HEADS = 1
PROJ_ROW_CHUNK = 256
CONV_COL_TILE = 256
MASKED = -1e30
LOG2E = math.log2(math.e)
VMEM_LIMIT = 56 * 1024 * 1024
MLP_VMEM_LIMIT = 62 * 1024 * 1024
MLP_VMEM_BUDGET = 60 * 1024 * 1024


def _params(*sem):
    return pltpu.CompilerParams(dimension_semantics=sem, vmem_limit_bytes=VMEM_LIMIT)


def _tile(n, pref):
    if n <= pref:
        return n
    t = pref
    while n % t:
        t -= LANE
    assert t > 0
    return t


def _rmsnorm_kernel(x_ref, g_ref, o_ref):
    x = x_ref[...]
    ms = jnp.mean(x * x, axis=-1, keepdims=True)
    o_ref[...] = (x * lax.rsqrt(ms + EPS) * g_ref[...]).astype(o_ref.dtype)


def _rmsnorm(x, g):
    T, D = x.shape
    tm = _tile(T, 512)
    return pl.pallas_call(
        _rmsnorm_kernel,
        grid=(T // tm,),
        in_specs=[pl.BlockSpec((tm, D), lambda i: (i, 0)),
                  pl.BlockSpec((1, D), lambda i: (0, 0))],
        out_specs=pl.BlockSpec((tm, D), lambda i: (i, 0)),
        out_shape=jax.ShapeDtypeStruct((T, D), BF16),
        compiler_params=_params("parallel"),
        name="rmsnorm",
    )(x, g.reshape(1, D).astype(F32))


def _cast_weights(w_refs, wb_ref):
    @pl.when(pl.program_id(1) == 0)
    def _():
        for n, w_ref in enumerate(w_refs):
            wb_ref[n] = w_ref[...].astype(wb_ref.dtype)


def _proj_heads_kernel(h_ref, w_ref, *rest, epilogue):
    p_ref, o_ref = rest[0], rest[-1]
    tm = h_ref.shape[0]
    rc = min(tm, PROJ_ROW_CHUNK)
    for r in range(tm // rc):
        rows = slice(r * rc, (r + 1) * rc)
        acc = jnp.dot(h_ref[rows, :], w_ref[...], preferred_element_type=F32)
        for g in range(o_ref.shape[0]):
            cols = slice(g * LANE, (g + 1) * LANE)
            a = acc[:, cols]
            if epilogue == "norm":
                ms = jnp.mean(a * a, axis=-1, keepdims=True)
                a = a * lax.rsqrt(ms + EPS) * p_ref[...]
            elif epilogue == "gate":
                a = jax.nn.sigmoid(a + p_ref[:, cols])
            o_ref[g, rows, :] = a.astype(o_ref.dtype)


def _proj_heads(h, w, *, seg_cols, gains=None, bias=None):
    T, D = h.shape
    n_cols = w.shape[1]
    tm = _tile(T, 2048)
    tn = _tile(seg_cols, 2048)
    tps = seg_cols // tn
    if gains is not None:
        epilogue, p, p_spec = "norm", gains, pl.BlockSpec((None, 1, LANE), lambda j, i: (j // tps, 0, 0))
    elif bias is not None:
        epilogue, p, p_spec = "gate", bias, pl.BlockSpec((1, tn), lambda j, i: (0, j))
    else:
        epilogue, p, p_spec = "plain", None, None
    return pl.pallas_call(
        functools.partial(_proj_heads_kernel, epilogue=epilogue),
        grid=(n_cols // tn, T // tm),
        in_specs=[pl.BlockSpec((tm, D), lambda j, i: (i, 0)),
                  pl.BlockSpec((D, tn), lambda j, i: (0, j))] + ([] if p is None else [p_spec]),
        out_specs=pl.BlockSpec((tn // LANE, tm, LANE), lambda j, i: (j, i, 0)),
        out_shape=jax.ShapeDtypeStruct((n_cols // LANE, T, LANE), BF16),
        compiler_params=_params("arbitrary", "arbitrary"),
        name="proj_heads_" + epilogue,
    )(h, w, *([] if p is None else [p]))


def _proj_conv_kernel(h_ref, wbg_ref, wcg_ref, wxc_ref, wgc_ref, b_ref, cw_ref, *rest, cast_plan):
    n_src = sum(cast_plan)
    o_ref, wb_ref = rest[n_src], rest[-1]
    srcs = iter(rest[:n_src])
    for dst_ref, n in zip(rest[n_src + 1:-1], cast_plan):
        width = dst_ref.shape[1] // n
        for k in range(n):
            dst_ref[:, k * width:(k + 1) * width] = next(srcs)[...].astype(dst_ref.dtype)
    _cast_weights([wbg_ref, wcg_ref, wxc_ref, wgc_ref], wb_ref)
    S = h_ref.shape[0]
    rc = min(S, PROJ_ROW_CHUNK)
    row = lax.broadcasted_iota(jnp.int32, (rc, 1), 0)
    tail = jnp.zeros((F32_SUBLANES, o_ref.shape[1]), F32)
    for r0 in range(0, S, rc):
        rows = slice(r0, r0 + rc)
        hc = h_ref[rows, :]
        bg, cg, xc, gc = [jnp.dot(hc, wb_ref[n], preferred_element_type=F32) for n in range(4)]
        u = cg * xc
        prev1, prev2 = tail[F32_SUBLANES - 1:, :], tail[F32_SUBLANES - 2:F32_SUBLANES - 1, :]
        u1 = jnp.where(row == 0, prev1, pltpu.roll(u, 1, 0))
        u2 = jnp.where(row == 0, prev2, jnp.where(row == 1, prev1, pltpu.roll(u, 2, 0)))
        conv = cw_ref[0:1, :] * u2 + cw_ref[1:2, :] * u1 + cw_ref[2:3, :] * u
        o_ref[rows, :] = (jax.nn.sigmoid(gc + b_ref[...]) * (bg * conv)).astype(o_ref.dtype)
        tail = u[rc - F32_SUBLANES:, :]


def _proj_conv(h, w_in, w_gate, layer, b_gate, conv_w, to_cast=(), *, seq, bg_col, cg_col, xc_col, gc_col):
    T, D = h.shape
    tn = _tile(D, CONV_COL_TILE)
    n_seq = T // seq
    steps = (D // tn) * n_seq

    def w_spec(col0):
        return pl.BlockSpec((None, D, tn), lambda j, i: (layer, 0, col0 // tn + j))

    slab_in, slab_args, slab_out, slab_shape = [], [], [], []
    for sources in to_cast:
        R, n_cols = sources[0][0].shape[1], sources[0][2]
        assert R % (steps * 2 * F32_SUBLANES) == 0, "row slabs must hold whole packed bf16 register tiles"
        for w, col0, nc in sources:
            assert w.shape[1] == R and nc == n_cols and col0 % nc == 0
            slab_in.append(pl.BlockSpec((None, R // steps, nc),
                                        lambda j, i, c=col0 // nc: (layer, j * n_seq + i, c)))
            slab_args.append(w)
        slab_out.append(pl.BlockSpec((R // steps, n_cols * len(sources)), lambda j, i: (j * n_seq + i, 0)))
        slab_shape.append(jax.ShapeDtypeStruct((R, n_cols * len(sources)), BF16))
    out = pl.pallas_call(
        functools.partial(_proj_conv_kernel, cast_plan=tuple(len(sources) for sources in to_cast)),
        grid=(D // tn, n_seq),
        in_specs=[pl.BlockSpec((seq, D), lambda j, i: (i, 0)),
                  w_spec(bg_col), w_spec(cg_col), w_spec(xc_col), w_spec(gc_col),
                  pl.BlockSpec((1, tn), lambda j, i: (0, gc_col // tn + j)),
                  pl.BlockSpec((3, tn), lambda j, i: (0, j))] + slab_in,
        out_specs=[pl.BlockSpec((seq, tn), lambda j, i: (i, j))] + slab_out,
        out_shape=[jax.ShapeDtypeStruct((T, D), BF16)] + slab_shape,
        scratch_shapes=[pltpu.VMEM((4, D, tn), BF16)],
        compiler_params=_params("arbitrary", "arbitrary"),
        name="proj_conv",
    )(h, w_in, w_in, w_in, w_gate, b_gate.reshape(1, -1).astype(F32), conv_w.astype(F32), *slab_args)
    return out[0], out[1:]


def _attn_a_bias_tile(rel_bias):
    H = rel_bias.shape[0]
    table = rel_bias.astype(F32)
    n_far = A_LEFT_CHUNKS * CHUNK - A_MAX_REL + CHUNK - 1
    far = jnp.broadcast_to(table[:, -1:], (H, n_far))
    near = table[:, :A_MAX_REL - CHUNK:-1]
    f = jnp.concatenate([far, near, jnp.zeros((H, 1), F32)], axis=1)
    L = f.shape[1] - 1
    windows = jnp.tile(f, (1, CHUNK))[:, :CHUNK * L].reshape(H, CHUNK, L)
    t = windows[:, :, CHUNK - 1:]
    masked = jnp.full((H, CHUNK, CHUNK), MASKED, F32)
    t = t * LOG2E
    return jnp.concatenate([jnp.concatenate([t, masked], axis=2),
                            jnp.concatenate([masked, t], axis=2)], axis=1)


def _attn_a_steps(q_ref, k_ref, v_ref, bias_ref, gate_ref, o_ref, vx_ref):
    S = q_ref.shape[0]
    n_blocks = S // A_QBLOCK
    vx_ref[:, :HEAD_DIM] = v_ref[...]
    vx_ref[:, HEAD_DIM:] = jnp.ones((S, HEAD_DIM), vx_ref.dtype)

    def band(blk):
        k1 = (blk + 1) * A_QBLOCK
        return max(0, k1 - A_BAND), k1

    def scores(blk):
        k0, k1 = band(blk)
        b0 = A_BAND - (k1 - k0)
        s = lax.dot_general(q_ref[blk * A_QBLOCK:k1, :], k_ref[k0:k1, :], (((1,), (1,)), ((), ())),
                            preferred_element_type=F32)
        return s + bias_ref[:, b0:]

    s_next = scores(0)
    for blk in range(n_blocks):
        s = s_next
        if blk + 1 < n_blocks:
            s_next = scores(blk + 1)
        k0, k1 = band(blk)
        e = jnp.exp2(s - jnp.max(s, axis=-1, keepdims=True))
        ox = jnp.dot(e.astype(BF16), vx_ref[k0:k1, :], preferred_element_type=F32)
        rows = slice(blk * A_QBLOCK, k1)
        o_ref[rows, :] = (ox[:, :HEAD_DIM] / ox[:, HEAD_DIM:]).astype(o_ref.dtype) * gate_ref[rows, :]
        yield


def _attn_b_steps(lam_ref, q_ref, k_ref, v_ref, g_ref, gate_ref, o_ref, *, out_scale):
    S = q_ref.shape[1]
    lam = lam_ref[0]
    v = jnp.concatenate([v_ref[0], v_ref[1]], axis=-1)
    chunk = lax.broadcasted_iota(jnp.int32, (B_QBLOCK, HEAD_DIM), 0) // CHUNK
    lane = lax.broadcasted_iota(jnp.int32, (B_QBLOCK, HEAD_DIM), 1)
    q_hot = jnp.where(lane == chunk, 1.0, 0.0).astype(q_ref.dtype)
    k_hot = jnp.where((lane < B_QBLOCK // CHUNK) & (chunk > lane), MASKED, 0.0).astype(k_ref.dtype)
    gs = g_ref[...] * out_scale
    nt = (((1,), (1,)), ((), ()))

    def scores(blk):
        q0, q1 = blk * B_QBLOCK, (blk + 1) * B_QBLOCK
        out = []
        for m in range(2):
            q = q_ref[m, q0:q1, :]
            sd = lax.dot_general(jnp.concatenate([q, q_hot], axis=1),
                                 jnp.concatenate([k_ref[m, q0:q1, :], k_hot], axis=1), nt,
                                 preferred_element_type=F32)
            sl = lax.dot_general(q, k_ref[m, 0:q0, :], nt, preferred_element_type=F32) if blk else None
            out.append((sl, sd))
        return out

    def softmax_parts(sl, sd):
        mx = jnp.max(sd, axis=-1, keepdims=True)
        el = None
        if sl is not None:
            mx = jnp.maximum(mx, jnp.max(sl, axis=-1, keepdims=True))
            el = jnp.exp2(sl - mx)
        ed = jnp.exp2(sd - mx)
        l = jnp.sum(ed, axis=-1, keepdims=True)
        if sl is not None:
            l = l + jnp.sum(el, axis=-1, keepdims=True)
        return el, ed, l

    n_blocks = S // B_QBLOCK
    s_next = scores(0)
    for blk in range(n_blocks):
        q0, q1 = blk * B_QBLOCK, (blk + 1) * B_QBLOCK
        (sl1, sd1), (sl2, sd2) = s_next
        if blk + 1 < n_blocks:
            s_next = scores(blk + 1)
        yield
        el1, ed1, l1 = softmax_parts(sl1, sd1)
        el2, ed2, l2 = softmax_parts(sl2, sd2)
        r = lam * l1 / l2
        ob = jnp.dot((ed1 - ed2 * r).astype(BF16), v[q0:q1, :], preferred_element_type=F32)
        if blk:
            ob = ob + jnp.dot((el1 - el2 * r).astype(BF16), v[0:q0, :], preferred_element_type=F32)
        inv_l1 = 1.0 / l1
        ms = jnp.mean(ob * ob, axis=-1, keepdims=True) * (inv_l1 * inv_l1)
        y = (ob * (inv_l1 * lax.rsqrt(ms + EPS)) * gs).astype(o_ref.dtype)
        for half in range(2):
            cols = slice(half * HEAD_DIM, (half + 1) * HEAD_DIM)
            o_ref[q0:q1, cols] = y[:, cols] * gate_ref[half, q0:q1, :]
        yield


def _attn_kernel(lam_ref, qa_ref, ka_ref, va_ref, bias_ref, ga_ref, qb_ref, kb_ref, vb_ref, g_ref, gb_ref,
                 oa_ref, ob_ref, vx_ref, *, out_scale):
    S = qa_ref.shape[1]
    for i in range(ob_ref.shape[0]):
        pair = pl.ds(2 * i, 2)
        for _ in _attn_b_steps(lam_ref, qb_ref.at[pair], kb_ref.at[pair], vb_ref.at[pair], g_ref, gb_ref.at[pair],
                               ob_ref.at[i], out_scale=out_scale):
            pass
    a_progs = [_attn_a_steps(qa_ref.at[i], ka_ref.at[i], va_ref.at[i], bias_ref.at[i], ga_ref.at[i],
                             oa_ref.at[i], vx_ref.at[i]) for i in range(qa_ref.shape[0])]
    for _ in range(S // A_QBLOCK):
        for prog in a_progs:
            next(prog)


def _attention(qk, vv, gates, bias_tile, lam, subln_g, *, batch, seq, b_heads, out_scale):
    T = qk.shape[1]
    nb = math.gcd(ATTN_B_HEADS, b_heads)
    steps = b_heads // nb

    def groups(seg):
        return pl.BlockSpec((2 * nb, seq, HEAD_DIM), lambda b, h: (seg * steps + h, b, 0))

    return pl.pallas_call(
        functools.partial(_attn_kernel, out_scale=out_scale),
        grid=(batch, steps),
        in_specs=[pl.BlockSpec(memory_space=pltpu.SMEM),
                  groups(0), groups(1), groups(0),
                  pl.BlockSpec((2 * nb, A_QBLOCK, A_BAND), lambda b, h: (h, 0, 0)), groups(0),
                  groups(2), groups(3), groups(1),
                  pl.BlockSpec((1, 2 * HEAD_DIM), lambda b, h: (0, 0)), groups(1)],
        out_specs=[pl.BlockSpec((2 * nb, seq, HEAD_DIM), lambda b, h: (h, b, 0)),
                   pl.BlockSpec((nb, seq, 2 * HEAD_DIM), lambda b, h: (h, b, 0))],
        out_shape=[jax.ShapeDtypeStruct((2 * b_heads, T, HEAD_DIM), BF16),
                   jax.ShapeDtypeStruct((b_heads, T, 2 * HEAD_DIM), BF16)],
        scratch_shapes=[pltpu.VMEM((2 * nb, seq, 2 * HEAD_DIM), BF16)],
        compiler_params=_params("parallel", "parallel"),
        name="attention",
    )(lam, qk, qk, vv, bias_tile, gates, qk, qk, vv, subln_g.reshape(1, 2 * HEAD_DIM).astype(F32), gates)


def _merge_out_kernel(x_ref, oa_ref, ob_ref, oc_ref, wo_ref, g2_ref, x1_ref, h2_ref, m_ref):
    D = x_ref.shape[1]
    for g in range(D // LANE):
        sl = slice(g * LANE, (g + 1) * LANE)
        half = slice((g % 2) * LANE, (g % 2 + 1) * LANE)
        m_ref[:, sl] = oa_ref[g] + ob_ref[g // 2, :, half] + oc_ref[:, sl]
    x1 = x_ref[...] + jnp.dot(m_ref[...], wo_ref[...], preferred_element_type=F32)
    x1_ref[...] = x1
    ms = jnp.mean(x1 * x1, axis=-1, keepdims=True)
    h2_ref[...] = (x1 * lax.rsqrt(ms + EPS) * g2_ref[...]).astype(h2_ref.dtype)


def _merge_out(x, out_a, out_b, out_c, w_o, g2):
    T, D = x.shape
    tm = _tile(T, 512)
    row = lambda i: (i, 0)
    once = pl.Buffered(1)
    return pl.pallas_call(
        _merge_out_kernel,
        grid=(T // tm,),
        in_specs=[pl.BlockSpec((tm, D), row),
                  pl.BlockSpec((out_a.shape[0], tm, HEAD_DIM), lambda i: (0, i, 0)),
                  pl.BlockSpec((out_b.shape[0], tm, 2 * HEAD_DIM), lambda i: (0, i, 0)),
                  pl.BlockSpec((tm, D), row),
                  pl.BlockSpec((D, D), lambda i: (0, 0), pipeline_mode=once),
                  pl.BlockSpec((1, D), lambda i: (0, 0), pipeline_mode=once)],
        out_specs=[pl.BlockSpec((tm, D), row), pl.BlockSpec((tm, D), row)],
        out_shape=[jax.ShapeDtypeStruct((T, D), F32), jax.ShapeDtypeStruct((T, D), BF16)],
        scratch_shapes=[pltpu.VMEM((tm, D), BF16)],
        compiler_params=_params("parallel"),
        name="merge_out",
    )(x, out_a, out_b, out_c, w_o, g2.reshape(1, D).astype(F32))


def _mlp_kernel(x1_ref, h2_ref, w1_ref, w2_ref, *rest, with_next):
    x2_ref = rest[1] if with_next else rest[0]
    f = pl.program_id(1)
    tm = h2_ref.shape[0]
    rc = min(tm, PROJ_ROW_CHUNK)
    chunks = [slice(r * rc, (r + 1) * rc) for r in range(tm // rc)]

    @pl.when(f == 0)
    def _():
        x2_ref[...] = x1_ref[...]

    for rows in chunks:
        a = jnp.dot(h2_ref[rows, :], w1_ref[...], preferred_element_type=F32)
        a = jnp.square(jnp.maximum(a, 0.0)).astype(BF16)
        x2_ref[rows, :] += jnp.dot(a, w2_ref[...], preferred_element_type=F32)

    if with_next:
        @pl.when(f == pl.num_programs(1) - 1)
        def _():
            x2 = x2_ref[...]
            ms = jnp.mean(x2 * x2, axis=-1, keepdims=True)
            rest[2][...] = (x2 * lax.rsqrt(ms + EPS) * rest[0][...]).astype(BF16)


def _mlp_ff_tile(tm, D, FF, with_next):
    rc = min(tm, PROJ_ROW_CHUNK)
    rows = tm * D * (4 + 2 + 4 + (2 if with_next else 0))
    for tf in (2048, 1024, 512, 256, LANE):
        if FF % tf:
            continue
        weights = 2 * D * tf * 2
        temps = rc * tf * (4 + 2)
        if 2 * (rows + weights) + temps <= MLP_VMEM_BUDGET:
            return tf
    return _tile(FF, LANE)


def _mlp(x1, h2, w1, w2, g_next):
    T, D = x1.shape
    FF = w1.shape[1]
    with_next = g_next is not None
    tm = _tile(T, 512)
    tf = _mlp_ff_tile(tm, D, FF, with_next)
    row = lambda i, f: (i, 0)
    in_specs = [pl.BlockSpec((tm, D), row),
                pl.BlockSpec((tm, D), row),
                pl.BlockSpec((D, tf), lambda i, f: (0, f)),
                pl.BlockSpec((tf, D), lambda i, f: (f, 0))]
    args = [x1, h2, w1, w2]
    out_specs = [pl.BlockSpec((tm, D), row)]
    out_shape = [jax.ShapeDtypeStruct((T, D), F32)]
    if with_next:
        in_specs.append(pl.BlockSpec((1, D), lambda i, f: (0, 0)))
        args.append(g_next.reshape(1, D).astype(F32))
        out_specs.append(pl.BlockSpec((tm, D), row))
        out_shape.append(jax.ShapeDtypeStruct((T, D), BF16))
    out = pl.pallas_call(
        functools.partial(_mlp_kernel, with_next=with_next),
        grid=(T // tm, FF // tf),
        in_specs=in_specs,
        out_specs=out_specs,
        out_shape=out_shape,
        compiler_params=pltpu.CompilerParams(dimension_semantics=("parallel", "arbitrary"),
                                             vmem_limit_bytes=MLP_VMEM_LIMIT),
        name="mlp",
    )(*args)
    return (out[0], out[1]) if with_next else (out[0], None)


def kernel(x, norm1_g, w_in, w_gate, b_gate, a_qn_g, a_kn_g, a_rel_bias, b_qn_g, b_kn_g,
           b_lq1, b_lk1, b_lq2, b_lk2, b_subln_g, c_conv_w, w_o, norm2_g, w_mlp1, w_mlp2):
    B, S, D = x.shape
    depth = w_in.shape[0]
    T = B * S
    b_heads = D // (2 * HEAD_DIM)
    assert S % B_QBLOCK == 0 and D % (2 * HEAD_DIM) == 0 and w_in.shape[2] == 9 * D
    scale = HEAD_DIM ** -0.5 * LOG2E

    xf = x.reshape(T, D).astype(F32)
    w_in, w_gate = w_in.astype(F32), w_gate.astype(F32)
    w_o, w_mlp1, w_mlp2 = w_o.astype(F32), w_mlp1.astype(F32), w_mlp2.astype(F32)

    h = _rmsnorm(xf, norm1_g[0])
    for l in range(depth):
        gains = jnp.stack([a_qn_g[l].astype(F32) * scale, a_kn_g[l].astype(F32),
                           b_qn_g[l].astype(F32) * scale, b_kn_g[l].astype(F32)]).reshape(4, 1, HEAD_DIM)
        FF = w_mlp1.shape[2]
        out_c, (w_o_b, w1_b, w2_b, wqk_b, wv_b, wg_b) = _proj_conv(
            h, w_in, w_gate, l, b_gate[l], c_conv_w[l],
            [[(w_o, 0, D)], [(w_mlp1, 0, FF)], [(w_mlp2, 0, D)],
             [(w_in, 0, D), (w_in, D, D), (w_in, 3 * D, D), (w_in, 4 * D, D)],
             [(w_in, 2 * D, D), (w_in, 5 * D, D)], [(w_gate, 0, 2 * D)]],
            seq=S, bg_col=6 * D, cg_col=7 * D, xc_col=8 * D, gc_col=2 * D)
        qk = _proj_heads(h, wqk_b, seg_cols=D, gains=gains)
        vv = _proj_heads(h, wv_b, seg_cols=2 * D)
        gates = _proj_heads(h, wg_b, seg_cols=2 * D,
                            bias=b_gate[l, :2 * D].reshape(1, -1).astype(F32))

        lambda_init = 0.8 - 0.6 * math.exp(-0.3 * l)
        lam = (jnp.exp(jnp.sum(b_lq1[l].astype(F32) * b_lk1[l].astype(F32)))
               - jnp.exp(jnp.sum(b_lq2[l].astype(F32) * b_lk2[l].astype(F32))) + lambda_init).reshape(1)
        out_a, out_b = _attention(qk, vv, gates, _attn_a_bias_tile(a_rel_bias[l]), lam, b_subln_g[l],
                                  batch=B, seq=S, b_heads=b_heads, out_scale=1.0 - lambda_init)

        x1, h2 = _merge_out(xf, out_a, out_b, out_c, w_o_b, norm2_g[l])
        xf, h = _mlp(x1, h2, w1_b, w2_b, norm1_g[l + 1] if l + 1 < depth else None)
    return xf.reshape(B, S, D).astype(x.dtype)
```
